```python
import math
import jax
import jax.numpy as jnp
from jax import lax
import numpy as np

D_MODEL = 1024
BATCH = 4
SEQ = 4096
DEPTH = 4

CTX_LEN = 256
GRID_W = 64

DA_HEADS = 4
DA_HEAD_DIM = 64
DA_V_DIM = 2 * DA_HEAD_DIM
DA_QK_WIDTH = DA_HEADS * 2 * DA_HEAD_DIM
DA_WIDTH = DA_HEADS * DA_V_DIM
Q_BLOCK = 128
ROPE_THETA = 10000.0

GLA_HEADS = 4
GLA_DK = 64
GLA_DV = 128
GLA_K_WIDTH = GLA_HEADS * GLA_DK
GLA_WIDTH = GLA_HEADS * GLA_DV
GLA_GATE_RANK = 16
GLA_GATE_TAU = 16.0
GLA_CHUNK = 64

HY_WIDTH = 512
HY_POS_DIM = 33
HY_HIDDEN = 64
HY_TARGET = 1e-2
HY_FAST_PCT = 0.3
HY_SLOW_PCT = 1.5

N_BRANCH = 3
BRANCH_WIDTH = 512
D_FF = 2816
LN_EPS = 1e-5
DEEPNORM_ALPHA = (2.0 * DEPTH) ** 0.25
DEEPNORM_BETA = (8.0 * DEPTH) ** -0.25

PROJ_SPLITS = (DA_QK_WIDTH, DA_QK_WIDTH, DA_WIDTH,
               GLA_K_WIDTH, GLA_K_WIDTH, GLA_WIDTH, GLA_WIDTH, 2 * GLA_GATE_RANK,
               3 * HY_WIDTH, N_BRANCH * D_MODEL)
D_IN_PROJ = 7712

kernel_name = 'hybrid_diffattn_gla_hyena_deepnorm_block'


def _layer_norm(x):
    xf = x.astype(jnp.float32)
    mu = jnp.mean(xf, axis=-1, keepdims=True)
    var = jnp.mean(jnp.square(xf - mu), axis=-1, keepdims=True)
    return ((xf - mu) * lax.rsqrt(var + LN_EPS)).astype(x.dtype)


def _rms_norm(x, g):
    xf = x.astype(jnp.float32)
    y = xf * lax.rsqrt(jnp.mean(jnp.square(xf), axis=-1, keepdims=True) + LN_EPS)
    return (y * g).astype(x.dtype)


def _modulate(x, shift, scale):
    return _layer_norm(x) * (1.0 + scale) + shift


def _post_norm(x, y, gate, g, b):
    return _layer_norm(DEEPNORM_ALPHA * x + gate * y) * g + b


def _dwconv3(x, w, b):
    xp = jnp.pad(x, ((0, 0), (1, 1), (0, 0)))
    return xp[:, :-2] * w[0] + xp[:, 1:-1] * w[1] + xp[:, 2:] * w[2] + b


def _split_cols(p):
    idx, acc = [], 0
    for s in PROJ_SPLITS[:-1]:
        acc += s
        idx.append(acc)
    return jnp.split(p, idx, axis=-1)


def _axial_rope(x):
    length = x.shape[1]
    n_rows = length // GRID_W
    rows = jnp.repeat(jnp.arange(n_rows), GRID_W).astype(jnp.float32)
    cols = (jnp.arange(length) % GRID_W).astype(jnp.float32)
    n_freq = DA_HEAD_DIM // 4
    inv = ROPE_THETA ** (-jnp.arange(n_freq, dtype=jnp.float32) / n_freq)

    def rot(xh, pos):
        ang = pos[:, None] * inv
        cos = jnp.cos(ang)[None, :, None, None, :]
        sin = jnp.sin(ang)[None, :, None, None, :]
        x1, x2 = jnp.split(xh.astype(jnp.float32), 2, axis=-1)
        return jnp.concatenate([x1 * cos - x2 * sin, x2 * cos + x1 * sin], axis=-1)

    xr, xc = jnp.split(x, 2, axis=-1)
    return jnp.concatenate([rot(xr, rows), rot(xc, cols)], axis=-1).astype(x.dtype)


def _diff_attn_block(q, k, v, lam):
    s = jnp.einsum('bqhmd,bkhmd->bhmqk', q, k).astype(jnp.float32) * (DA_HEAD_DIM ** -0.5)
    p = jax.nn.softmax(s, axis=-1)
    a = p[:, :, 0] - lam * p[:, :, 1]
    return jnp.einsum('bhqk,bkhe->bqhe', a.astype(v.dtype), v)


def _flip(t):
    return jnp.flip(t, axis=2)


def _gla_states(k, v, log_a, s0):
    bsz, heads, length, dk = k.shape
    n = length // GLA_CHUNK
    kc = k.reshape(bsz, heads, n, GLA_CHUNK, dk)
    vc = v.reshape(bsz, heads, n, GLA_CHUNK, v.shape[-1])
    b = jnp.cumsum(log_a.reshape(bsz, heads, n, GLA_CHUNK, dk), axis=3)
    b_end = b[:, :, :, -1:, :]
    u = jnp.einsum('bhnck,bhncv->bhnkv', kc * jnp.exp(b_end - b), vc)
    chunk_decay = jnp.exp(b_end[:, :, :, 0, :])

    def step(s, inp):
        d_n, u_n = inp
        return d_n[..., None] * s + u_n, s

    s_final, s_start = lax.scan(step, s0, (jnp.moveaxis(chunk_decay, 2, 0), jnp.moveaxis(u, 2, 0)))
    return b, jnp.moveaxis(s_start, 0, 2), s_final


def _gla_output(q, k, v, b, s_start):
    bsz, heads, length, dk = q.shape
    n = length // GLA_CHUNK
    qc = q.reshape(bsz, heads, n, GLA_CHUNK, dk)
    kc = k.reshape(bsz, heads, n, GLA_CHUNK, dk)
    vc = v.reshape(bsz, heads, n, GLA_CHUNK, v.shape[-1])
    ref = b[:, :, :, GLA_CHUNK // 2:GLA_CHUNK // 2 + 1, :]
    att = jnp.einsum('bhnck,bhnsk->bhncs', qc * jnp.exp(b - ref), kc * jnp.exp(ref - b))
    tri = jnp.tril(jnp.ones((GLA_CHUNK, GLA_CHUNK), dtype=bool))
    att = jnp.where(tri, att, 0.0)
    o = (jnp.einsum('bhncs,bhnsv->bhncv', att, vc)
         + jnp.einsum('bhnck,bhnkv->bhncv', qc * jnp.exp(b), s_start))
    return o.reshape(bsz, heads, length, -1)


def _hyena_filters(length, p):
    t = jnp.linspace(0.0, 1.0, length, dtype=jnp.float32)[:, None]
    bands = (HY_POS_DIM - 1) // 2
    w = 2.0 * math.pi * jnp.arange(length, dtype=jnp.float32)[:, None] / length
    f = jnp.linspace(1e-4, bands - 1, bands, dtype=jnp.float32)[None, :]
    feat = jnp.concatenate([t, jnp.cos(f * w), -jnp.sin(f * w)], axis=-1)
    h = jnp.sin(p['hy_freq1'] * (feat @ p['hy_w1'] + p['hy_b1']))
    h = jnp.sin(p['hy_freq2'] * (h @ p['hy_w2'] + p['hy_b2']))
    h = (h @ p['hy_w3']).reshape(length, 2, HY_WIDTH)
    max_decay = math.log(HY_TARGET) / HY_FAST_PCT
    min_decay = math.log(HY_TARGET) / HY_SLOW_PCT
    deltas = jnp.abs(jnp.linspace(min_decay, max_decay, HY_WIDTH, dtype=jnp.float32))
    h = h * jnp.exp(-t * deltas)[:, None, :]
    return h[:, 0], h[:, 1]


def _long_conv(u, h_f, h_b):
    length = u.shape[1]
    n_fft = 2 * length
    taps = jnp.concatenate([h_f, jnp.zeros((1, h_f.shape[1]), h_f.dtype), h_b[:0:-1]], axis=0)
    uf = jnp.fft.rfft(u.astype(jnp.float32), n=n_fft, axis=1)
    tf = jnp.fft.rfft(taps.astype(jnp.float32), n=n_fft, axis=0)
    y = jnp.fft.irfft(uf * tf[None], n=n_fft, axis=1)[:, :length]
    return y.astype(u.dtype)


def _token_mixer(h_lat, h_ctx, p, layer_idx, with_ctx):
    bsz, length, _ = h_lat.shape
    c_len = h_ctx.shape[1]
    (a_q, a_k, a_v, g_q, g_k, g_v, g_r, g_z, hy_in, gate_logits) = _split_cols(h_lat @ p['w_in'])
    (ca_q, ca_k, ca_v, cg_q, cg_k, cg_v, cg_r, cg_z, chy_in, cgate_logits) = _split_cols(h_ctx @ p['w_in'])

    lam_init = 0.8 - 0.6 * math.exp(-0.3 * layer_idx)
    lam_p = p['da_lambda'].astype(jnp.float32)
    lam = jnp.exp(jnp.sum(lam_p[0] * lam_p[1])) - jnp.exp(jnp.sum(lam_p[2] * lam_p[3])) + lam_init
    q_lat = _axial_rope(a_q.reshape(bsz, length, DA_HEADS, 2, DA_HEAD_DIM))
    k_lat = _axial_rope(a_k.reshape(bsz, length, DA_HEADS, 2, DA_HEAD_DIM))
    v_lat = a_v.reshape(bsz, length, DA_HEADS, DA_V_DIM)
    k_ctx = ca_k.reshape(bsz, c_len, DA_HEADS, 2, DA_HEAD_DIM)
    v_ctx = ca_v.reshape(bsz, c_len, DA_HEADS, DA_V_DIM)
    k_all = jnp.concatenate([k_ctx, k_lat], axis=1)
    v_all = jnp.concatenate([v_ctx, v_lat], axis=1)
    n_blk = length // Q_BLOCK
    q_blocks = jnp.moveaxis(q_lat.reshape(bsz, n_blk, Q_BLOCK, DA_HEADS, 2, DA_HEAD_DIM), 1, 0)
    o_blocks = lax.map(lambda qb: _diff_attn_block(qb, k_all, v_all, lam), q_blocks)
    o_a = jnp.moveaxis(o_blocks, 0, 1).reshape(bsz, length, DA_HEADS, DA_V_DIM)
    y_a = (_rms_norm(o_a, p['da_norm_g']) * (1.0 - lam_init)).reshape(bsz, length, DA_WIDTH)

    def gla_inputs(q, k, v, z, n):
        q = q.reshape(bsz, n, GLA_HEADS, GLA_DK).transpose(0, 2, 1, 3).astype(jnp.float32) * (GLA_DK ** -0.5)
        k = k.reshape(bsz, n, GLA_HEADS, GLA_DK).transpose(0, 2, 1, 3).astype(jnp.float32)
        v = v.reshape(bsz, n, GLA_HEADS, GLA_DV).transpose(0, 2, 1, 3).astype(jnp.float32)
        z_f, z_b = jnp.split(z.astype(jnp.float32), 2, axis=-1)

        def log_decay(zd, d):
            la = jax.nn.log_sigmoid(zd @ p['gla_w_gate'][d] + p['gla_b_gate'][d]) / GLA_GATE_TAU
            return la.astype(jnp.float32).reshape(bsz, n, GLA_HEADS, GLA_DK).transpose(0, 2, 1, 3)

        return q, k, v, log_decay(z_f, 0), log_decay(z_b, 1)

    def gla_merge(o, r, n):
        o = _rms_norm(o.transpose(0, 2, 1, 3), p['gla_norm_g']).reshape(bsz, n, GLA_WIDTH)
        return o.astype(r.dtype) * jax.nn.silu(r)

    q_b, k_b, v_b, la_f, la_b = gla_inputs(g_q, g_k, g_v, g_z, length)
    cq_b, ck_b, cv_b, cla_f, cla_b = gla_inputs(cg_q, cg_k, cg_v, cg_z, c_len)
    s0 = jnp.zeros((bsz, GLA_HEADS, GLA_DK, GLA_DV), jnp.float32)
    cb_f, cst_f, cs_f = _gla_states(ck_b, cv_b, cla_f, s0)
    cb_b, cst_b, cs_b = _gla_states(_flip(ck_b), _flip(cv_b), _flip(cla_b), s0)
    b_f, st_f, _ = _gla_states(k_b, v_b, la_f, cs_f)
    b_b, st_b, _ = _gla_states(_flip(k_b), _flip(v_b), _flip(la_b), cs_b)
    o_b = (_gla_output(q_b, k_b, v_b, b_f, st_f)
           + _flip(_gla_output(_flip(q_b), _flip(k_b), _flip(v_b), b_b, st_b)))
    y_b = gla_merge(o_b, g_r, length)

    def hyena(hy, n):
        hy = _dwconv3(hy, p['hy_conv_w'], p['hy_conv_b'])
        x0, x1, vv = jnp.split(hy, 3, axis=-1)
        u = x1 * vv
        h_f, h_b = _hyena_filters(n, p)
        return x0 * (_long_conv(u, h_f, h_b) + p['hy_skip'] * u)

    y_c = hyena(hy_in, length)

    def merge(ya, yb, yc, logits, n):
        ys = jnp.stack([ya, yb.astype(ya.dtype), yc.astype(ya.dtype)], axis=2)
        proj = jnp.einsum('bngc,gcd->bngd', ys, p['w_branch'])
        gates = jax.nn.sigmoid(logits.reshape(bsz, n, N_BRANCH, D_MODEL))
        return jnp.sum(gates * proj, axis=2) @ p['w_out']

    out_lat = merge(y_a, y_b, y_c, gate_logits, length)
    if not with_ctx:
        return out_lat, None

    o_ca = _diff_attn_block(ca_q.reshape(bsz, c_len, DA_HEADS, 2, DA_HEAD_DIM), k_ctx, v_ctx, lam)
    y_ca = (_rms_norm(o_ca, p['da_norm_g']) * (1.0 - lam_init)).reshape(bsz, c_len, DA_WIDTH)
    o_cb = (_gla_output(cq_b, ck_b, cv_b, cb_f, cst_f)
            + _flip(_gla_output(_flip(cq_b), _flip(ck_b), _flip(cv_b), cb_b, cst_b)))
    y_cb = gla_merge(o_cb, cg_r, c_len)
    y_cc = hyena(chy_in, c_len)
    out_ctx = merge(y_ca, y_cb, y_cc, cgate_logits, c_len)
    return out_lat, out_ctx


def _conv_ffn(h, p):
    up = _dwconv3(h @ p['ffn_w_up'], p['ffn_conv_w'], p['ffn_conv_b'])
    a, g = jnp.split(up, 2, axis=-1)
    return (jax.nn.gelu(g) * a) @ p['ffn_w_down']


def setup_inputs(seed: int = 0) -> dict:
    key = jax.random.key(seed)
    ks = jax.random.split(key, 32)

    def nrm(i, shape, scale):
        return scale * jax.random.normal(ks[i], shape, jnp.float32)

    L = DEPTH
    D = D_MODEL
    return {
        'x': nrm(0, (BATCH, SEQ, D), 1.0),
        'c': nrm(1, (BATCH, D), 1.0),
        'ctx': nrm(2, (BATCH, CTX_LEN, D), 1.0),
        'c_ctx': nrm(3, (D,), 1.0),
        'w_ada': nrm(4, (L, D, 6 * D), 0.5 * D ** -0.5),
        'b_ada': nrm(5, (L, 6 * D), 0.02),
        'w_in': nrm(6, (L, D, D_IN_PROJ), D ** -0.5),
        'da_lambda': nrm(7, (L, 4, DA_HEAD_DIM), 0.1),
        'da_norm_g': 1.0 + nrm(8, (L, DA_V_DIM), 0.02),
        'gla_w_gate': nrm(9, (L, 2, GLA_GATE_RANK, GLA_K_WIDTH), GLA_GATE_RANK ** -0.5),
        'gla_b_gate': nrm(10, (L, 2, GLA_K_WIDTH), 0.1),
        'gla_norm_g': 1.0 + nrm(11, (L, GLA_DV), 0.02),
        'hy_conv_w': nrm(12, (L, 3, 3 * HY_WIDTH), 3 ** -0.5),
        'hy_conv_b': nrm(13, (L, 3 * HY_WIDTH), 0.02),
        'hy_w1': nrm(14, (L, HY_POS_DIM, HY_HIDDEN), HY_POS_DIM ** -0.5),
        'hy_b1': nrm(15, (L, HY_HIDDEN), 0.1),
        'hy_freq1': 1.0 + nrm(16, (L, HY_HIDDEN), 0.1),
        'hy_w2': nrm(17, (L, HY_HIDDEN, HY_HIDDEN), HY_HIDDEN ** -0.5),
        'hy_b2': nrm(18, (L, HY_HIDDEN), 0.1),
        'hy_freq2': 1.0 + nrm(19, (L, HY_HIDDEN), 0.1),
        'hy_w3': nrm(20, (L, HY_HIDDEN, 2 * HY_WIDTH), 0.05 * HY_HIDDEN ** -0.5),
        'hy_skip': 1.0 + nrm(21, (L, HY_WIDTH), 0.1),
        'w_branch': nrm(22, (L, N_BRANCH, BRANCH_WIDTH, D), BRANCH_WIDTH ** -0.5),
        'w_out': nrm(23, (L, D, D), DEEPNORM_BETA * D ** -0.5),
        'ln1_g': 1.0 + nrm(24, (L, D), 0.02),
        'ln1_b': nrm(25, (L, D), 0.02),
        'ffn_w_up': nrm(26, (L, D, 2 * D_FF), D ** -0.5),
        'ffn_conv_w': nrm(27, (L, 3, 2 * D_FF), 3 ** -0.5),
        'ffn_conv_b': nrm(28, (L, 2 * D_FF), 0.02),
        'ffn_w_down': nrm(29, (L, D_FF, D), DEEPNORM_BETA * D_FF ** -0.5),
        'ln2_g': 1.0 + nrm(30, (L, D), 0.02),
        'ln2_b': nrm(31, (L, D), 0.02),
    }


def reference(x, c, ctx, c_ctx, w_ada, b_ada, w_in, da_lambda, da_norm_g, gla_w_gate, gla_b_gate,
              gla_norm_g, hy_conv_w, hy_conv_b, hy_w1, hy_b1, hy_freq1, hy_w2, hy_b2, hy_freq2, hy_w3,
              hy_skip, w_branch, w_out, ln1_g, ln1_b, ffn_w_up, ffn_conv_w, ffn_conv_b, ffn_w_down,
              ln2_g, ln2_b):
    x_lat, x_ctx = x, ctx
    for l in range(DEPTH):
        with_ctx = l < DEPTH - 1
        p = {
            'w_in': w_in[l], 'da_lambda': da_lambda[l], 'da_norm_g': da_norm_g[l],
            'gla_w_gate': gla_w_gate[l], 'gla_b_gate': gla_b_gate[l], 'gla_norm_g': gla_norm_g[l],
            'hy_conv_w': hy_conv_w[l], 'hy_conv_b': hy_conv_b[l], 'hy_w1': hy_w1[l], 'hy_b1': hy_b1[l],
            'hy_freq1': hy_freq1[l], 'hy_w2': hy_w2[l], 'hy_b2': hy_b2[l], 'hy_freq2': hy_freq2[l],
            'hy_w3': hy_w3[l], 'hy_skip': hy_skip[l], 'w_branch': w_branch[l], 'w_out': w_out[l],
            'ffn_w_up': ffn_w_up[l], 'ffn_conv_w': ffn_conv_w[l], 'ffn_conv_b': ffn_conv_b[l],
            'ffn_w_down': ffn_w_down[l],
        }
        mod_lat = jax.nn.silu(c) @ w_ada[l] + b_ada[l]
        mod_ctx = jax.nn.silu(c_ctx) @ w_ada[l] + b_ada[l]
        sh1, sc1, g1, sh2, sc2, g2 = jnp.split(mod_lat[:, None, :], 6, axis=-1)
        csh1, csc1, cg1, csh2, csc2, cg2 = jnp.split(mod_ctx, 6)
        y_lat, y_ctx = _token_mixer(_modulate(x_lat, sh1, sc1), _modulate(x_ctx, csh1, csc1), p, l, with_ctx)
        x_lat = _post_norm(x_lat, y_lat, g1, ln1_g[l], ln1_b[l])
        x_lat = _post_norm(x_lat, _conv_ffn(_modulate(x_lat, sh2, sc2), p), g2, ln2_g[l], ln2_b[l])
        if with_ctx:
            x_ctx = _post_norm(x_ctx, y_ctx, cg1, ln1_g[l], ln1_b[l])
            x_ctx = _post_norm(x_ctx, _conv_ffn(_modulate(x_ctx, csh2, csc2), p), cg2, ln2_g[l], ln2_b[l])
    return x_lat
```

```python
import functools
import math

import numpy as np
import jax
import jax.numpy as jnp
from jax import lax
from jax.experimental import pallas as pl
from jax.experimental.pallas import tpu as pltpu

F32 = jnp.float32
BF16 = jnp.bfloat16
HIGHEST = lax.Precision.HIGHEST

D_MODEL = 1024
DEPTH = 4
GRID_W = 64
DA_HEADS = 4
DA_HEAD_DIM = 64
DA_V_DIM = 128
ROPE_THETA = 10000.0
GLA_HEADS = 4
GLA_DK = 64
GLA_DV = 128
GLA_K_WIDTH = 256
GLA_WIDTH = 512
GLA_GATE_RANK = 16
GLA_GATE_TAU = 16.0
GLA_CHUNK = 64
HY_WIDTH = 512
HY_POS_DIM = 33
HY_HIDDEN = 64
HY_TARGET = 1e-2
HY_FAST_PCT = 0.3
HY_SLOW_PCT = 1.5
N_BRANCH = 3
D_FF = 2816
LN_EPS = 1e-5
DEEPNORM_ALPHA = (2.0 * DEPTH) ** 0.25

NB_COLS = 6144
OFF_AQ, OFF_AK, OFF_AV, OFF_GQ, OFF_GK, OFF_GV, OFF_GR, OFF_GATE = 0, 512, 1024, 1536, 1792, 2048, 2560, 3072
NF_COLS = 1664
OFF_GZ = 1536

FFT_N1 = 128
VMEM_LIMIT = 56 * 1024 * 1024


def _cparams(sem):
    return pltpu.CompilerParams(dimension_semantics=sem, vmem_limit_bytes=VMEM_LIMIT)


def _layer_norm_f32(x):
    mu = jnp.mean(x, axis=-1, keepdims=True)
    xc = x - mu
    var = jnp.mean(xc * xc, axis=-1, keepdims=True)
    return xc * lax.rsqrt(var + LN_EPS)


def _ada_kernel(c_ref, w_ref, b_ref, o_ref):
    cc = c_ref[...]
    s = cc * jax.nn.sigmoid(cc)
    o_ref[0] = jnp.dot(s, w_ref[0], precision=HIGHEST, preferred_element_type=F32) + b_ref[0]


def _ada_all(cc, w_ada, b_ada):
    depth, d, n = w_ada.shape
    tn = 1536
    return pl.pallas_call(
        _ada_kernel,
        grid=(depth, n // tn),
        in_specs=[
            pl.BlockSpec((8, d), lambda l, j: (0, 0)),
            pl.BlockSpec((1, d, tn), lambda l, j: (l, 0, j)),
            pl.BlockSpec((1, 1, tn), lambda l, j: (l, 0, j)),
        ],
        out_specs=pl.BlockSpec((1, 8, tn), lambda l, j: (l, 0, j)),
        out_shape=jax.ShapeDtypeStruct((depth, 8, n), F32),
        compiler_params=_cparams(("parallel", "parallel")),
        name="ada",
    )(cc, w_ada, b_ada.reshape(depth, 1, n))


def _swap16(x):
    n = x.shape[-1]
    lane = lax.broadcasted_iota(jnp.int32, x.shape, x.ndim - 1)
    up = pltpu.roll(x, n - 16, axis=x.ndim - 1)
    dn = pltpu.roll(x, 16, axis=x.ndim - 1)
    return jnp.where((lane & 16) == 0, up, dn)


def _inproj_kernel(*refs, rope_tiles):
    if rope_tiles:
        x_ref, sh_ref, sc_ref, w_ref, cos_ref, sin_ref, o_ref, h_scr = refs
    else:
        x_ref, sh_ref, sc_ref, w_ref, o_ref, h_scr = refs
    n = pl.program_id(2)

    @pl.when(n == 0)
    def _():
        h = _layer_norm_f32(x_ref[0]) * (1.0 + sc_ref[0]) + sh_ref[0]
        h_scr[...] = h.astype(BF16)

    acc = jnp.dot(h_scr[...], w_ref[...], preferred_element_type=F32)
    if rope_tiles:
        @pl.when(n < rope_tiles)
        def _():
            reps = acc.shape[-1] // cos_ref.shape[-1]
            cos = jnp.concatenate([cos_ref[...]] * reps, axis=-1)
            sin = jnp.concatenate([sin_ref[...]] * reps, axis=-1)
            o_ref[0] = (acc * cos + _swap16(acc) * sin).astype(o_ref.dtype)

        @pl.when(n >= rope_tiles)
        def _():
            o_ref[0] = acc.astype(o_ref.dtype)
    else:
        o_ref[0] = acc.astype(o_ref.dtype)


def _inproj(x, shift, scale, w, out_dtype, tn, rope=None):
    bsz, length, d = x.shape
    n_cols = w.shape[1]
    tm = min(length, 1024)
    rope_tiles = 0
    in_specs = [
        pl.BlockSpec((1, tm, d), lambda b, i, j: (b, i, 0)),
        pl.BlockSpec((1, 1, d), lambda b, i, j: (b, 0, 0)),
        pl.BlockSpec((1, 1, d), lambda b, i, j: (b, 0, 0)),
        pl.BlockSpec((d, tn), lambda b, i, j: (0, j)),
    ]
    args = [x, shift, scale, w]
    if rope is not None:
        rope_tiles = (2 * DA_HEADS * 2 * DA_HEAD_DIM) // tn
        in_specs += [pl.BlockSpec((tm, 128), lambda b, i, j: (i, 0))] * 2
        args += list(rope)
    return pl.pallas_call(
        functools.partial(_inproj_kernel, rope_tiles=rope_tiles),
        grid=(bsz, length // tm, n_cols // tn),
        in_specs=in_specs,
        out_specs=pl.BlockSpec((1, tm, tn), lambda b, i, j: (b, i, j)),
        out_shape=jax.ShapeDtypeStruct((bsz, length, n_cols), out_dtype),
        scratch_shapes=[pltpu.VMEM((tm, d), BF16)],
        compiler_params=_cparams(("parallel", "parallel", "arbitrary")),
        name="inproj",
    )(*args)


def _rope_tables(length):
    n_freq = DA_HEAD_DIM // 4
    inv = ROPE_THETA ** (-jnp.arange(n_freq, dtype=F32) / n_freq)
    rows = jnp.repeat(jnp.arange(length // GRID_W), GRID_W).astype(F32)
    cols = (jnp.arange(length) % GRID_W).astype(F32)
    ang_r = rows[:, None] * inv
    ang_c = cols[:, None] * inv
    cos64 = jnp.concatenate([jnp.cos(ang_r), jnp.cos(ang_r), jnp.cos(ang_c), jnp.cos(ang_c)], axis=-1)
    sin64 = jnp.concatenate([-jnp.sin(ang_r), jnp.sin(ang_r), -jnp.sin(ang_c), jnp.sin(ang_c)], axis=-1)
    return jnp.concatenate([cos64, cos64], axis=-1), jnp.concatenate([sin64, sin64], axis=-1)


def _attn_kernel(*refs, n_src, tk, lam_init):
    lam_ref, g_ref, q_ref = refs[:3]
    kv_refs = refs[3:3 + 2 * n_src]
    o_ref = refs[3 + 2 * n_src]
    tq = q_ref.shape[1]

    lp = lam_ref[...]
    lam = (jnp.exp(jnp.sum(lp[0] * lp[1], axis=-1, keepdims=True))
           - jnp.exp(jnp.sum(lp[2] * lp[3], axis=-1, keepdims=True)) + lam_init)

    q = q_ref[0]
    lane = lax.broadcasted_iota(jnp.int32, q.shape, 1)
    zero = jnp.zeros_like(q)
    qs = jnp.concatenate([jnp.where(lane < DA_HEAD_DIM, q, zero),
                          jnp.where(lane >= DA_HEAD_DIM, q, zero)], axis=0)
    qs = qs * jnp.asarray(DA_HEAD_DIM ** -0.5, qs.dtype)

    m = jnp.full((2 * tq, 1), -1e30, F32)
    l = jnp.zeros((2 * tq, 1), F32)
    acc = jnp.zeros((2 * tq, DA_V_DIM), F32)
    for s_idx in range(n_src):
        k_ref, v_ref = kv_refs[2 * s_idx], kv_refs[2 * s_idx + 1]
        n_keys = k_ref.shape[1]
        tkk = min(tk, n_keys)

        def body(j, carry, k_ref=k_ref, v_ref=v_ref, tkk=tkk):
            m, l, acc = carry
            start = pl.multiple_of(j * tkk, tkk)
            kb = k_ref[0, pl.ds(start, tkk), :]
            vb = v_ref[0, pl.ds(start, tkk), :]
            s = lax.dot_general(qs, kb, (((1,), (1,)), ((), ())), preferred_element_type=F32)
            m_new = jnp.maximum(m, jnp.max(s, axis=-1, keepdims=True))
            alpha = jnp.exp(m - m_new)
            p = jnp.exp(s - m_new)
            l = alpha * l + jnp.sum(p, axis=-1, keepdims=True)
            acc = alpha * acc + jnp.dot(p.astype(BF16), vb, preferred_element_type=F32)
            return m_new, l, acc

        m, l, acc = lax.fori_loop(0, n_keys // tkk, body, (m, l, acc))

    o = acc / l
    a = o[:tq] - lam * o[tq:]
    y = a * lax.rsqrt(jnp.mean(a * a, axis=-1, keepdims=True) + LN_EPS) * g_ref[...]
    o_ref[0] = (y * (1.0 - lam_init)).astype(o_ref.dtype)


def _diff_attention(q_arr, kv_arrs, lam_p, norm_g, lam_init):
    bsz, lq, _ = q_arr.shape
    tq = min(lq, 256)
    in_specs = [
        pl.BlockSpec((4, 1, DA_HEAD_DIM), lambda b, h, i: (0, 0, 0)),
        pl.BlockSpec((1, DA_V_DIM), lambda b, h, i: (0, 0)),
        pl.BlockSpec((1, tq, 128), lambda b, h, i: (b, i, OFF_AQ // 128 + h)),
    ]
    args = [lam_p.reshape(4, 1, DA_HEAD_DIM), norm_g.reshape(1, DA_V_DIM), q_arr]
    for arr in kv_arrs:
        lk = arr.shape[1]
        in_specs += [pl.BlockSpec((1, lk, 128), lambda b, h, i: (b, 0, OFF_AK // 128 + h)),
                     pl.BlockSpec((1, lk, 128), lambda b, h, i: (b, 0, OFF_AV // 128 + h))]
        args += [arr, arr]
    return pl.pallas_call(
        functools.partial(_attn_kernel, n_src=len(kv_arrs), tk=512, lam_init=lam_init),
        grid=(bsz, DA_HEADS, lq // tq),
        in_specs=in_specs,
        out_specs=pl.BlockSpec((1, tq, 128), lambda b, h, i: (b, i, h)),
        out_shape=jax.ShapeDtypeStruct((bsz, lq, DA_HEADS * DA_V_DIM), BF16),
        compiler_params=_cparams(("parallel", "parallel", "parallel")),
        name="diff_attn",
    )(*args)


def _gla_kernel(*refs, reverse, final):
    if final:
        (q_ref, k_ref, v_ref, z_ref, wg_ref, bg_ref, tri_ref, s0_ref, oprev_ref, r_ref, g_ref,
         o_ref, sfin_ref, s_scr) = refs
    else:
        (q_ref, k_ref, v_ref, z_ref, wg_ref, bg_ref, tri_ref, s0_ref,
         o_ref, sfin_ref, s_scr) = refs
    j = pl.program_id(1)
    tb = q_ref.shape[1]
    nch = tb // GLA_CHUNK
    c = GLA_CHUNK

    @pl.when(j == 0)
    def _():
        s_scr[...] = s0_ref[0]

    pre = jnp.dot(z_ref[0], wg_ref[...], precision=HIGHEST, preferred_element_type=F32) + bg_ref[...]
    la = (jnp.minimum(pre, 0.0) - jnp.log1p(jnp.exp(-jnp.abs(pre)))) * (1.0 / GLA_GATE_TAU)
    b_all = jnp.dot(tri_ref[...], la, precision=HIGHEST, preferred_element_type=F32)

    row = lax.broadcasted_iota(jnp.int32, (GLA_HEADS * c, c), 0) % c
    col = lax.broadcasted_iota(jnp.int32, (GLA_HEADS * c, c), 1)
    causal = (row <= col) if reverse else (row >= col)
    hrow = lax.broadcasted_iota(jnp.int32, (GLA_K_WIDTH, GLA_WIDTH), 0) // GLA_DK
    hcol = lax.broadcasted_iota(jnp.int32, (GLA_K_WIDTH, GLA_WIDTH), 1) // GLA_DV
    blockdiag = hrow == hcol
    qlane_head = lax.broadcasted_iota(jnp.int32, (c, GLA_K_WIDTH), 1) // GLA_DK
    ones_c = jnp.ones((c, 128), F32)

    state = s_scr[...]
    order = range(nch - 1, -1, -1) if reverse else range(nch)
    for ci in order:
        r0 = ci * c
        bc = b_all[r0:r0 + c]
        lac = la[r0:r0 + c]
        if reverse:
            b_end, b_ref_row = bc[0:1], bc[c - 1 - c // 2:c - c // 2]
        else:
            b_end, b_ref_row = bc[c - 1:c], bc[c // 2:c // 2 + 1]
        qc = q_ref[0, r0:r0 + c, :].astype(F32) * (GLA_DK ** -0.5)
        kc = k_ref[0, r0:r0 + c, :].astype(F32)
        vc = v_ref[0, r0:r0 + c, :]
        qe = qc * jnp.exp(bc - b_ref_row)
        ke = (kc * jnp.exp(b_ref_row - bc)).astype(BF16)
        zq = jnp.zeros_like(qe)
        q_stack = jnp.concatenate([jnp.where(qlane_head == h, qe, zq) for h in range(GLA_HEADS)],
                                  axis=0).astype(BF16)
        att = lax.dot_general(q_stack, ke, (((1,), (1,)), ((), ())), preferred_element_type=F32)
        att = jnp.where(causal, att, 0.0).astype(BF16)
        o_intra = jnp.concatenate(
            [jnp.dot(att[h * c:(h + 1) * c], vc[:, h * GLA_DV:(h + 1) * GLA_DV], preferred_element_type=F32)
             for h in range(GLA_HEADS)], axis=-1)
        o_inter = jnp.dot((qc * jnp.exp(bc)).astype(BF16), state.astype(BF16), preferred_element_type=F32)
        o_chunk = o_intra + o_inter
        if final:
            o_chunk = o_chunk + oprev_ref[0, r0:r0 + c, :]
            parts = []
            for h in range(GLA_HEADS):
                oh = o_chunk[:, h * GLA_DV:(h + 1) * GLA_DV]
                parts.append(oh * lax.rsqrt(jnp.mean(oh * oh, axis=-1, keepdims=True) + LN_EPS) * g_ref[...])
            rr = r_ref[0, r0:r0 + c, :].astype(F32)
            o_ref[0, r0:r0 + c, :] = (jnp.concatenate(parts, axis=-1)
                                      * (rr * jax.nn.sigmoid(rr))).astype(o_ref.dtype)
        else:
            o_ref[0, r0:r0 + c, :] = o_chunk.astype(o_ref.dtype)
        kd_t = jnp.transpose(kc * jnp.exp(b_end - bc)).astype(BF16)
        upd = jnp.dot(kd_t, vc, preferred_element_type=F32)
        dec = jnp.exp(jnp.dot(jnp.transpose(lac), ones_c, precision=HIGHEST, preferred_element_type=F32))
        dec = jnp.concatenate([dec] * (GLA_WIDTH // 128), axis=-1)
        state = dec * state + jnp.where(blockdiag, upd, 0.0)
    s_scr[...] = state
    sfin_ref[0] = state


def _gla_pass(pb, pf, wg_pad, bg, tri, s0, reverse, final_args=None):
    bsz, length, _ = pb.shape
    tb = min(length, 512)
    nb = length // tb
    blk = (lambda j: nb - 1 - j) if reverse else (lambda j: j)
    in_specs = [
        pl.BlockSpec((1, tb, GLA_K_WIDTH), lambda b, j: (b, blk(j), OFF_GQ // GLA_K_WIDTH)),
        pl.BlockSpec((1, tb, GLA_K_WIDTH), lambda b, j: (b, blk(j), OFF_GK // GLA_K_WIDTH)),
        pl.BlockSpec((1, tb, GLA_WIDTH), lambda b, j: (b, blk(j), OFF_GV // GLA_WIDTH)),
        pl.BlockSpec((1, tb, 128), lambda b, j: (b, blk(j), OFF_GZ // 128)),
        pl.BlockSpec((128, GLA_K_WIDTH), lambda b, j: (0, 0)),
        pl.BlockSpec((1, GLA_K_WIDTH), lambda b, j: (0, 0)),
        pl.BlockSpec((tb, tb), lambda b, j: (0, 0)),
        pl.BlockSpec((1, GLA_K_WIDTH, GLA_WIDTH), lambda b, j: (b, 0, 0)),
    ]
    args = [pb, pb, pb, pf, wg_pad, bg, tri, s0]
    final = final_args is not None
    if final:
        o_prev, norm_g = final_args
        in_specs += [
            pl.BlockSpec((1, tb, GLA_WIDTH), lambda b, j: (b, blk(j), 0)),
            pl.BlockSpec((1, tb, GLA_WIDTH), lambda b, j: (b, blk(j), OFF_GR // GLA_WIDTH)),
            pl.BlockSpec((1, GLA_DV), lambda b, j: (0, 0)),
        ]
        args += [o_prev, pb, norm_g.reshape(1, GLA_DV)]
    return pl.pallas_call(
        functools.partial(_gla_kernel, reverse=reverse, final=final),
        grid=(bsz, nb),
        in_specs=in_specs,
        out_specs=[pl.BlockSpec((1, tb, GLA_WIDTH), lambda b, j: (b, blk(j), 0)),
                   pl.BlockSpec((1, GLA_K_WIDTH, GLA_WIDTH), lambda b, j: (b, 0, 0))],
        out_shape=[jax.ShapeDtypeStruct((bsz, length, GLA_WIDTH), BF16 if final else F32),
                   jax.ShapeDtypeStruct((bsz, GLA_K_WIDTH, GLA_WIDTH), F32)],
        scratch_shapes=[pltpu.VMEM((GLA_K_WIDTH, GLA_WIDTH), F32)],
        compiler_params=_cparams(("parallel", "arbitrary")),
        name="gla_bwd" if reverse else "gla_fwd",
    )(*args)


def _gla_consts(tb):
    tri = np.kron(np.eye(tb // GLA_CHUNK), np.tril(np.ones((GLA_CHUNK, GLA_CHUNK)))).astype(np.float32)
    return jnp.asarray(tri), jnp.asarray(tri.T.copy())


def _hy_filter_kernel(feat_ref, w1_ref, b1_ref, f1_ref, w2_ref, b2_ref, f2_ref, w3_ref, dl_ref, o_ref):
    feat = feat_ref[...]
    h = jnp.sin(f1_ref[...] * (jnp.dot(feat, w1_ref[...], precision=HIGHEST, preferred_element_type=F32)
                               + b1_ref[...]))
    h = jnp.sin(f2_ref[...] * (jnp.dot(h, w2_ref[...], precision=HIGHEST, preferred_element_type=F32)
                               + b2_ref[...]))
    h = jnp.dot(h, w3_ref[...], precision=HIGHEST, preferred_element_type=F32)
    win = jnp.exp(-feat[:, 0:1] * dl_ref[...])
    o_ref[0] = h[:, :HY_WIDTH] * win
    first = (lax.broadcasted_iota(jnp.int32, win.shape, 0) + pl.program_id(0) * win.shape[0]) == 0
    o_ref[1] = jnp.where(first, 0.0, h[:, HY_WIDTH:] * win)


def _hy_feats(length, lpad):
    t = jnp.linspace(0.0, 1.0, length, dtype=F32)[:, None]
    bands = (HY_POS_DIM - 1) // 2
    w = 2.0 * math.pi * jnp.arange(length, dtype=F32)[:, None] / length
    f = jnp.linspace(1e-4, bands - 1, bands, dtype=F32)[None, :]
    feat = jnp.concatenate([t, jnp.cos(f * w), -jnp.sin(f * w)], axis=-1)
    return jnp.zeros((lpad, 128), F32).at[:length, :HY_POS_DIM].set(feat)


def _hy_filters(feat, length, p):
    lpad = feat.shape[0]
    tl = min(lpad, 512)
    max_decay = math.log(HY_TARGET) / HY_FAST_PCT
    min_decay = math.log(HY_TARGET) / HY_SLOW_PCT
    deltas = jnp.abs(jnp.linspace(min_decay, max_decay, HY_WIDTH, dtype=F32))[None, :]
    w1 = jnp.zeros((128, HY_HIDDEN), F32).at[:HY_POS_DIM].set(p['hy_w1'])
    full = lambda shape: pl.BlockSpec(shape, lambda i: (0,) * len(shape))
    taps = pl.pallas_call(
        _hy_filter_kernel,
        grid=(lpad // tl,),
        in_specs=[pl.BlockSpec((tl, 128), lambda i: (i, 0)),
                  full((128, HY_HIDDEN)), full((1, HY_HIDDEN)), full((1, HY_HIDDEN)),
                  full((HY_HIDDEN, HY_HIDDEN)), full((1, HY_HIDDEN)), full((1, HY_HIDDEN)),
                  full((HY_HIDDEN, 2 * HY_WIDTH)), full((1, HY_WIDTH))],
        out_specs=pl.BlockSpec((2, tl, HY_WIDTH), lambda i: (0, i, 0)),
        out_shape=jax.ShapeDtypeStruct((2, lpad, HY_WIDTH), F32),
        compiler_params=_cparams(("parallel",)),
        name="hy_filter",
    )(feat, w1, p['hy_b1'].reshape(1, -1), p['hy_freq1'].reshape(1, -1), p['hy_w2'],
      p['hy_b2'].reshape(1, -1), p['hy_freq2'].reshape(1, -1), p['hy_w3'], deltas)
    if lpad > length:
        taps = jnp.where((jnp.arange(lpad) < length)[None, :, None], taps, 0.0)
    return taps


def _hy_pre_kernel(x0_ref, x1_ref, vv_ref, w0_ref, w1_ref, wv_ref, b0_ref, b1_ref, bv_ref, x0o_ref, u_ref):
    length = x0_ref.shape[1]
    lpad = u_ref.shape[1]
    row = lax.broadcasted_iota(jnp.int32, (length, x0_ref.shape[2]), 0)

    def conv(x_ref, w_ref, b_ref):
        x = x_ref[0]
        w = w_ref[...]
        xm = jnp.where(row == 0, 0.0, pltpu.roll(x, 1, axis=0))
        xp = jnp.where(row == length - 1, 0.0, pltpu.roll(x, length - 1, axis=0))
        return xm * w[0:1] + x * w[1:2] + xp * w[2:3] + b_ref[...]

    x0 = conv(x0_ref, w0_ref, b0_ref)
    u = conv(x1_ref, w1_ref, b1_ref) * conv(vv_ref, wv_ref, bv_ref)
    x0o_ref[0, 0:length, :] = x0
    u_ref[0, 0:length, :] = u
    if lpad > length:
        x0o_ref[0, length:lpad, :] = jnp.zeros((lpad - length, x0.shape[1]), F32)
        u_ref[0, length:lpad, :] = jnp.zeros((lpad - length, x0.shape[1]), F32)


def _hy_pre(pf, conv_w, conv_b, lpad):
    bsz, length, _ = pf.shape
    ct = 128
    nct = HY_WIDTH // ct
    xs = lambda g: pl.BlockSpec((1, length, ct), lambda b, j, g=g: (b, 0, g * nct + j))
    ws = lambda g: pl.BlockSpec((3, ct), lambda b, j, g=g: (0, g * nct + j))
    bs = lambda g: pl.BlockSpec((1, ct), lambda b, j, g=g: (0, g * nct + j))
    cb = conv_b.reshape(1, -1)
    return pl.pallas_call(
        _hy_pre_kernel,
        grid=(bsz, nct),
        in_specs=[xs(0), xs(1), xs(2), ws(0), ws(1), ws(2), bs(0), bs(1), bs(2)],
        out_specs=[pl.BlockSpec((1, lpad, ct), lambda b, j: (b, 0, j))] * 2,
        out_shape=[jax.ShapeDtypeStruct((bsz, lpad, HY_WIDTH), F32)] * 2,
        compiler_params=_cparams(("parallel", "parallel")),
        name="hy_pre",
    )(pf, pf, pf, conv_w, conv_w, conv_w, cb, cb, cb)


def _fft_consts(n2, nin):
    n1 = FFT_N1
    n = n1 * n2
    k2 = np.arange(n2)[:, None]
    m2 = np.arange(nin)[None, :]
    ang_a = 2.0 * np.pi * (k2 * m2 % n2) / n2
    fa = np.concatenate([np.cos(ang_a), -np.sin(ang_a)], axis=0)
    ga = np.concatenate([np.cos(ang_a).T, -np.sin(ang_a).T], axis=1) / n
    k1 = np.arange(n1)[None, :, None]
    j1 = np.arange(n1)[None, None, :]
    kk2 = np.arange(n2)[:, None, None]
    ang_b = 2.0 * np.pi * ((j1 * k1 * n2 + j1 * kk2) % n) / n
    mr, mi = np.cos(ang_b), -np.sin(ang_b)
    big = np.concatenate([np.concatenate([mr, -mi], axis=2), np.concatenate([mi, mr], axis=2)], axis=1)
    big_t = np.transpose(big, (0, 2, 1))
    to = lambda a: jnp.asarray(a.astype(np.float32))
    return to(fa), to(ga), to(big), to(big_t)


def _fft_a_kernel(f_ref, x_ref, o_ref):
    o_ref[0] = jnp.dot(f_ref[...], x_ref[0], precision=HIGHEST, preferred_element_type=F32)


def _fft_stage_a(x, fa):
    g, rows, ch = x.shape
    nin = fa.shape[1]
    lanes = FFT_N1 * ch
    tn = 8192
    xv = x.reshape(g, nin, lanes)
    return pl.pallas_call(
        _fft_a_kernel,
        grid=(g, lanes // tn),
        in_specs=[pl.BlockSpec(fa.shape, lambda b, j: (0, 0)),
                  pl.BlockSpec((1, nin, tn), lambda b, j: (b, 0, j))],
        out_specs=pl.BlockSpec((1, fa.shape[0], tn), lambda b, j: (b, 0, j)),
        out_shape=jax.ShapeDtypeStruct((g, fa.shape[0], lanes), F32),
        compiler_params=_cparams(("parallel", "parallel")),
        name="fft_stage_a",
    )(fa, xv)


def _fft_spec_kernel(m_ref, a_ref, o_ref):
    big = m_ref[0]
    xf = jnp.dot(big, jnp.concatenate([a_ref[0, 0, 0], a_ref[0, 1, 0]], axis=0),
                 precision=HIGHEST, preferred_element_type=F32)
    xb = jnp.dot(big, jnp.concatenate([a_ref[1, 0, 0], a_ref[1, 1, 0]], axis=0),
                 precision=HIGHEST, preferred_element_type=F32)
    o_ref[0, 0] = xf[:FFT_N1] + xb[:FFT_N1]
    o_ref[1, 0] = xf[FFT_N1:] - xb[FFT_N1:]


def _filter_spectrum(a_taps, big, n2):
    ch = a_taps.shape[-1] // FFT_N1
    av = a_taps.reshape(2, 2, n2, FFT_N1, ch)
    return pl.pallas_call(
        _fft_spec_kernel,
        grid=(n2,),
        in_specs=[pl.BlockSpec((1, 2 * FFT_N1, 2 * FFT_N1), lambda k: (k, 0, 0)),
                  pl.BlockSpec((2, 2, 1, FFT_N1, ch), lambda k: (0, 0, k, 0, 0))],
        out_specs=pl.BlockSpec((2, 1, FFT_N1, ch), lambda k: (0, k, 0, 0)),
        out_shape=jax.ShapeDtypeStruct((2, n2, FFT_N1, ch), F32),
        compiler_params=_cparams(("parallel",)),
        name="fft_filter_spec",
    )(big, av)


def _fft_conv_kernel(m_ref, mt_ref, h_ref, a_ref, o_ref):
    x = jnp.dot(m_ref[0], jnp.concatenate([a_ref[0, 0, 0], a_ref[0, 1, 0]], axis=0),
                precision=HIGHEST, preferred_element_type=F32)
    xr, xi = x[:FFT_N1], x[FFT_N1:]
    hr, hi = h_ref[0, 0], h_ref[1, 0]
    y = jnp.concatenate([xr * hr - xi * hi, xr * hi + xi * hr], axis=0)
    t = jnp.dot(mt_ref[0], y, precision=HIGHEST, preferred_element_type=F32)
    o_ref[0, 0, 0] = t[:FFT_N1]
    o_ref[0, 1, 0] = t[FFT_N1:]


def _fft_conv_mid(a_sig, spec, big, big_t, n2):
    bsz = a_sig.shape[0]
    ch = a_sig.shape[-1] // FFT_N1
    av = a_sig.reshape(bsz, 2, n2, FFT_N1, ch)
    out = pl.pallas_call(
        _fft_conv_kernel,
        grid=(n2, bsz),
        in_specs=[pl.BlockSpec((1, 2 * FFT_N1, 2 * FFT_N1), lambda k, b: (k, 0, 0)),
                  pl.BlockSpec((1, 2 * FFT_N1, 2 * FFT_N1), lambda k, b: (k, 0, 0)),
                  pl.BlockSpec((2, 1, FFT_N1, ch), lambda k, b: (0, k, 0, 0)),
                  pl.BlockSpec((1, 2, 1, FFT_N1, ch), lambda k, b: (b, 0, k, 0, 0))],
        out_specs=pl.BlockSpec((1, 2, 1, FFT_N1, ch), lambda k, b: (b, 0, k, 0, 0)),
        out_shape=jax.ShapeDtypeStruct((bsz, 2, n2, FFT_N1, ch), F32),
        compiler_params=_cparams(("parallel", "parallel")),
        name="fft_conv_mid",
    )(big, big_t, spec, av)
    return out.reshape(bsz, 2 * n2, FFT_N1 * ch)


def _fft_out_kernel(g_ref, t_ref, x0_ref, u_ref, skip_ref, o_ref):
    y = jnp.dot(g_ref[...], t_ref[0], precision=HIGHEST, preferred_element_type=F32)
    u = u_ref[0]
    o_ref[0] = x0_ref[0] * (y + skip_ref[...] * u)


def _fft_out(t, ga, x0, u, skip):
    bsz, rows, ch = u.shape
    nin = ga.shape[0]
    lanes = FFT_N1 * ch
    tn = 8192
    skip_t = jnp.tile(skip.reshape(1, ch), (1, tn // ch))
    out = pl.pallas_call(
        _fft_out_kernel,
        grid=(bsz, lanes // tn),
        in_specs=[pl.BlockSpec(ga.shape, lambda b, j: (0, 0)),
                  pl.BlockSpec((1, ga.shape[1], tn), lambda b, j: (b, 0, j)),
                  pl.BlockSpec((1, nin, tn), lambda b, j: (b, 0, j)),
                  pl.BlockSpec((1, nin, tn), lambda b, j: (b, 0, j)),
                  pl.BlockSpec((1, tn), lambda b, j: (0, 0))],
        out_specs=pl.BlockSpec((1, nin, tn), lambda b, j: (b, 0, j)),
        out_shape=jax.ShapeDtypeStruct((bsz, nin, lanes), F32),
        compiler_params=_cparams(("parallel", "parallel")),
        name="fft_out",
    )(ga, t, x0.reshape(bsz, nin, lanes), u.reshape(bsz, nin, lanes), skip_t)
    return out.reshape(bsz, rows, ch)


def _hyena(pf, p, consts, length):
    fa, ga, big, big_t, feat, n2, lpad = consts
    taps = _hy_filters(feat, length, p)
    spec = _filter_spectrum(_fft_stage_a(taps, fa), big, n2)
    x0, u = _hy_pre(pf, p['hy_conv_w'], p['hy_conv_b'], lpad)
    t = _fft_conv_mid(_fft_stage_a(u, fa), spec, big, big_t, n2)
    y = _fft_out(t, ga, x0, u, p['hy_skip'])
    return y if lpad == length else y[:, :length]


def _hyena_consts(length):
    if length >= 2048:
        lpad, n2 = length, 2 * length // FFT_N1
    else:
        lpad = max(4 * length, 1024)
        n2 = lpad // FFT_N1
    nin = lpad // FFT_N1
    return _fft_consts(n2, nin) + (_hy_feats(length, lpad), n2, lpad)


def _merge_kernel(ya_ref, yb_ref, yc_ref, lg_ref, x_ref, gate_ref, lng_ref, lnb_ref, wb_ref, wo_ref, o_ref):
    lg = lg_ref[0].astype(F32)
    ys = (ya_ref[0], yb_ref[0], yc_ref[0].astype(BF16))
    mix = None
    for g in range(N_BRANCH):
        proj = jnp.dot(ys[g], wb_ref[g], preferred_element_type=F32)
        term = jax.nn.sigmoid(lg[:, g * D_MODEL:(g + 1) * D_MODEL]) * proj
        mix = term if mix is None else mix + term
    y = jnp.dot(mix.astype(BF16), wo_ref[...], preferred_element_type=F32)
    z = DEEPNORM_ALPHA * x_ref[0] + gate_ref[0] * y
    o_ref[0] = _layer_norm_f32(z) * lng_ref[...] + lnb_ref[...]


def _merge(ya, yb, yc, pb, x, gate, ln_g, ln_b, w_branch, w_out):
    bsz, length, d = x.shape
    tm = min(length, 512)
    return pl.pallas_call(
        _merge_kernel,
        grid=(bsz, length // tm),
        in_specs=[pl.BlockSpec((1, tm, 512), lambda b, i: (b, i, 0)),
                  pl.BlockSpec((1, tm, 512), lambda b, i: (b, i, 0)),
                  pl.BlockSpec((1, tm, 512), lambda b, i: (b, i, 0)),
                  pl.BlockSpec((1, tm, N_BRANCH * d), lambda b, i: (b, i, OFF_GATE // (N_BRANCH * d))),
                  pl.BlockSpec((1, tm, d), lambda b, i: (b, i, 0)),
                  pl.BlockSpec((1, 1, d), lambda b, i: (b, 0, 0)),
                  pl.BlockSpec((1, d), lambda b, i: (0, 0)),
                  pl.BlockSpec((1, d), lambda b, i: (0, 0)),
                  pl.BlockSpec((N_BRANCH, 512, d), lambda b, i: (0, 0, 0)),
                  pl.BlockSpec((d, d), lambda b, i: (0, 0))],
        out_specs=pl.BlockSpec((1, tm, d), lambda b, i: (b, i, 0)),
        out_shape=jax.ShapeDtypeStruct((bsz, length, d), F32),
        compiler_params=_cparams(("parallel", "parallel")),
        name="merge",
    )(ya, yb, yc, pb, x, gate, ln_g.reshape(1, d), ln_b.reshape(1, d), w_branch, w_out)


FF_CHUNK = 256


def _gelu_tanh(x):
    return 0.5 * x * (1.0 + jnp.tanh(math.sqrt(2.0 / math.pi) * (x + 0.044715 * (x * x * x))))


def _ffn_kernel(xp_ref, x_ref, xn_ref, sh_ref, sc_ref, gate_ref, lng_ref, lnb_ref,
                wup_ref, cw_ref, cb_ref, wdn_ref, o_ref, h_scr, acc_scr):
    i = pl.program_id(1)
    nt = pl.num_programs(1)
    tm = x_ref.shape[1]
    sc, sh = 1.0 + sc_ref[0], sh_ref[0]
    x = x_ref[0]
    h_scr[0:8, :] = _layer_norm_f32(xp_ref[0]) * sc + sh
    h_scr[8:8 + tm, :] = _layer_norm_f32(x) * sc + sh
    h_scr[8 + tm:16 + tm, :] = _layer_norm_f32(xn_ref[0]) * sc + sh
    hext = h_scr[...].astype(BF16)

    row = lax.broadcasted_iota(jnp.int32, (tm, FF_CHUNK), 0)
    keep_prev = jnp.logical_or(row > 0, i > 0)
    keep_next = jnp.logical_or(row < tm - 1, i < nt - 1)

    def conv(up, w, b):
        um = jnp.where(keep_prev, up[7:7 + tm], 0.0)
        un = jnp.where(keep_next, up[9:9 + tm], 0.0)
        return um * w[0:1] + up[8:8 + tm] * w[1:2] + un * w[2:3] + b

    n_chunks = D_FF // FF_CHUNK
    for cidx in range(n_chunks):
        ca = cidx * FF_CHUNK
        cg = D_FF + cidx * FF_CHUNK
        up_a = jnp.dot(hext, wup_ref[:, ca:ca + FF_CHUNK], preferred_element_type=F32)
        up_g = jnp.dot(hext, wup_ref[:, cg:cg + FF_CHUNK], preferred_element_type=F32)
        a = conv(up_a, cw_ref[:, ca:ca + FF_CHUNK], cb_ref[:, ca:ca + FF_CHUNK])
        g = conv(up_g, cw_ref[:, cg:cg + FF_CHUNK], cb_ref[:, cg:cg + FF_CHUNK])
        act = (_gelu_tanh(g) * a).astype(BF16)
        part = jnp.dot(act, wdn_ref[ca:ca + FF_CHUNK, :], preferred_element_type=F32)
        if cidx == 0:
            acc_scr[...] = part
        else:
            acc_scr[...] += part
    z = DEEPNORM_ALPHA * x + gate_ref[0] * acc_scr[...]
    o_ref[0] = _layer_norm_f32(z) * lng_ref[...] + lnb_ref[...]


def _ffn(x, shift, scale, gate, ln_g, ln_b, w_up, conv_w, conv_b, w_down):
    bsz, length, d = x.shape
    tm = min(length, 512)
    nt = length // tm
    r8 = tm // 8
    last8 = length // 8 - 1
    vec = pl.BlockSpec((1, 1, d), lambda b, i: (b, 0, 0))
    row = pl.BlockSpec((1, d), lambda b, i: (0, 0))
    return pl.pallas_call(
        _ffn_kernel,
        grid=(bsz, nt),
        in_specs=[pl.BlockSpec((1, 8, d), lambda b, i: (b, jnp.maximum(i * r8 - 1, 0), 0)),
                  pl.BlockSpec((1, tm, d), lambda b, i: (b, i, 0)),
                  pl.BlockSpec((1, 8, d), lambda b, i: (b, jnp.minimum((i + 1) * r8, last8), 0)),
                  vec, vec, vec, row, row,
                  pl.BlockSpec((d, 2 * D_FF), lambda b, i: (0, 0)),
                  pl.BlockSpec((3, 2 * D_FF), lambda b, i: (0, 0)),
                  pl.BlockSpec((1, 2 * D_FF), lambda b, i: (0, 0)),
                  pl.BlockSpec((D_FF, d), lambda b, i: (0, 0))],
        out_specs=pl.BlockSpec((1, tm, d), lambda b, i: (b, i, 0)),
        out_shape=jax.ShapeDtypeStruct((bsz, length, d), F32),
        scratch_shapes=[pltpu.VMEM((tm + 16, d), F32), pltpu.VMEM((tm, d), F32)],
        compiler_params=_cparams(("parallel", "parallel")),
        name="conv_ffn",
    )(x, x, x, shift, scale, gate, ln_g.reshape(1, d), ln_b.reshape(1, d),
      w_up, conv_w, conv_b.reshape(1, -1), w_down)


def _prep_w_in(w_in):
    a_q, a_k, a_v, g_q, g_k, g_v, g_r, g_z, hy, gate = jnp.split(
        w_in, [512, 1024, 1536, 1792, 2048, 2560, 3072, 3104, 4640], axis=-1)
    wb = jnp.concatenate([a_q, a_k, a_v, g_q, g_k, g_v, g_r, gate], axis=-1).astype(BF16)
    wf = jnp.concatenate([hy, g_z, jnp.zeros((w_in.shape[0], NF_COLS - OFF_GZ - 2 * GLA_GATE_RANK), w_in.dtype)],
                         axis=-1).astype(BF16)
    return wb, wf


def _prep_gate(w_gate, b_gate):
    pads = []
    for d in range(2):
        m = jnp.zeros((128, GLA_K_WIDTH), F32).at[d * GLA_GATE_RANK:(d + 1) * GLA_GATE_RANK].set(w_gate[d])
        pads.append(m)
    return pads, [b_gate[0].reshape(1, -1), b_gate[1].reshape(1, -1)]


def _mixer_stream(x, shift, scale, p, wb, wf, rope):
    pb = _inproj(x, shift, scale, wb, BF16, 512, rope=rope)
    pf = _inproj(x, shift, scale, wf, F32, NF_COLS)
    return pb, pf


def kernel(x, c, ctx, c_ctx, w_ada, b_ada, w_in, da_lambda, da_norm_g, gla_w_gate, gla_b_gate, gla_norm_g,
           hy_conv_w, hy_conv_b, hy_w1, hy_b1, hy_freq1, hy_w2, hy_b2, hy_freq2, hy_w3, hy_skip, w_branch,
           w_out, ln1_g, ln1_b, ffn_w_up, ffn_conv_w, ffn_conv_b, ffn_w_down, ln2_g, ln2_b):
    bsz, length, d = x.shape
    c_len = ctx.shape[1]
    depth = w_ada.shape[0]

    cc = jnp.zeros((8, d), F32).at[:bsz].set(c).at[bsz].set(c_ctx)
    mod = _ada_all(cc, w_ada, b_ada)

    rope = _rope_tables(length)
    hy_lat = _hyena_consts(length)
    hy_ctx = _hyena_consts(c_len)
    tri_lat = _gla_consts(min(length, 512))
    tri_ctx = _gla_consts(min(c_len, 512))
    zero_state = jnp.zeros((bsz, GLA_K_WIDTH, GLA_WIDTH), F32)

    x_lat, x_ctx = x, ctx
    for l in range(depth):
        with_ctx = l < depth - 1
        lam_init = 0.8 - 0.6 * math.exp(-0.3 * l)
        p = {'hy_conv_w': hy_conv_w[l], 'hy_conv_b': hy_conv_b[l], 'hy_w1': hy_w1[l], 'hy_b1': hy_b1[l],
             'hy_freq1': hy_freq1[l], 'hy_w2': hy_w2[l], 'hy_b2': hy_b2[l], 'hy_freq2': hy_freq2[l],
             'hy_w3': hy_w3[l], 'hy_skip': hy_skip[l]}
        wb, wf = _prep_w_in(w_in[l])
        wg_pad, bg = _prep_gate(gla_w_gate[l], gla_b_gate[l])
        wbr = w_branch[l].astype(BF16)
        wo = w_out[l].astype(BF16)
        wup = ffn_w_up[l].astype(BF16)
        wdn = ffn_w_down[l].astype(BF16)

        m_lat = mod[l, :bsz].reshape(bsz, 1, 6 * d)
        m_ctx = jnp.broadcast_to(mod[l, bsz].reshape(1, 1, 6 * d), (bsz, 1, 6 * d))
        sh1, sc1, g1, sh2, sc2, g2 = [m_lat[..., k * d:(k + 1) * d] for k in range(6)]
        csh1, csc1, cg1, csh2, csc2, cg2 = [m_ctx[..., k * d:(k + 1) * d] for k in range(6)]

        pb, pf = _mixer_stream(x_lat, sh1, sc1, p, wb, wf, rope)
        cpb, cpf = _mixer_stream(x_ctx, csh1, csc1, p, wb, wf, None)

        y_a = _diff_attention(pb, [cpb, pb], da_lambda[l], da_norm_g[l], lam_init)
        co_f, cs_f = _gla_pass(cpb, cpf, wg_pad[0], bg[0], tri_ctx[0], zero_state, False)
        if with_ctx:
            y_cb, cs_b = _gla_pass(cpb, cpf, wg_pad[1], bg[1], tri_ctx[1], zero_state, True,
                                   final_args=(co_f, gla_norm_g[l]))
        else:
            _, cs_b = _gla_pass(cpb, cpf, wg_pad[1], bg[1], tri_ctx[1], zero_state, True)
        o_f, _ = _gla_pass(pb, pf, wg_pad[0], bg[0], tri_lat[0], cs_f, False)
        y_b, _ = _gla_pass(pb, pf, wg_pad[1], bg[1], tri_lat[1], cs_b, True, final_args=(o_f, gla_norm_g[l]))
        y_c = _hyena(pf, p, hy_lat, length)

        x_lat = _merge(y_a, y_b, y_c, pb, x_lat, g1, ln1_g[l], ln1_b[l], wbr, wo)
        x_lat = _ffn(x_lat, sh2, sc2, g2, ln2_g[l], ln2_b[l], wup, ffn_conv_w[l], ffn_conv_b[l], wdn)
        if with_ctx:
            y_ca = _diff_attention(cpb, [cpb], da_lambda[l], da_norm_g[l], lam_init)
            y_cc = _hyena(cpf, p, hy_ctx, c_len)
            x_ctx = _merge(y_ca, y_cb, y_cc, cpb, x_ctx, cg1, ln1_g[l], ln1_b[l], wbr, wo)
            x_ctx = _ffn(x_ctx, csh2, csc2, cg2, ln2_g[l], ln2_b[l], wup, ffn_conv_w[l], ffn_conv_b[l], wdn)
    return x_lat
```

```python
import functools
import math

import numpy as np
import jax
import jax.numpy as jnp
from jax import lax
from jax.experimental import pallas as pl
from jax.experimental.pallas import tpu as pltpu

F32 = jnp.float32
BF16 = jnp.bfloat16
HIGHEST = lax.Precision.HIGHEST

D_MODEL = 1024
DEPTH = 4
GRID_W = 64
DA_HEADS = 4
DA_HEAD_DIM = 64
DA_V_DIM = 128
ROPE_THETA = 10000.0
GLA_HEADS = 4
GLA_DK = 64
GLA_DV = 128
GLA_K_WIDTH = 256
GLA_WIDTH = 512
GLA_GATE_RANK = 16
GLA_GATE_TAU = 16.0
GLA_CHUNK = 64
HY_WIDTH = 512
HY_POS_DIM = 33
HY_HIDDEN = 64
HY_TARGET = 1e-2
HY_FAST_PCT = 0.3
HY_SLOW_PCT = 1.5
N_BRANCH = 3
D_FF = 2816
LN_EPS = 1e-5
DEEPNORM_ALPHA = (2.0 * DEPTH) ** 0.25

NB_COLS = 7680
OFF_AQ, OFF_AK, OFF_AV, OFF_GQ, OFF_GK, OFF_GV, OFF_GR, OFF_GATE, OFF_HY = (
    0, 512, 1024, 1536, 1792, 2048, 2560, 3072, 6144)

LANES = 128
FFT_N1 = 128
SLAB_PITCH = 136
VMEM_LIMIT = 56 * 1024 * 1024


def _cparams(sem):
    return pltpu.CompilerParams(dimension_semantics=sem, vmem_limit_bytes=VMEM_LIMIT)


def _layer_norm_f32(x):
    mu = jnp.mean(x, axis=-1, keepdims=True)
    xc = x - mu
    var = jnp.mean(xc * xc, axis=-1, keepdims=True)
    return xc * lax.rsqrt(var + LN_EPS)


def _ada_kernel(c_ref, w_ref, b_ref, o_ref):
    cc = c_ref[...]
    s = cc * jax.nn.sigmoid(cc)
    o_ref[0] = jnp.dot(s, w_ref[0], precision=HIGHEST, preferred_element_type=F32) + b_ref[0]


def _ada_all(cc, w_ada, b_ada):
    depth, d, n = w_ada.shape
    tn = 1536
    return pl.pallas_call(
        _ada_kernel,
        grid=(depth, n // tn),
        in_specs=[
            pl.BlockSpec((8, d), lambda l, j: (0, 0)),
            pl.BlockSpec((1, d, tn), lambda l, j: (l, 0, j)),
            pl.BlockSpec((1, 1, tn), lambda l, j: (l, 0, j)),
        ],
        out_specs=pl.BlockSpec((1, 8, tn), lambda l, j: (l, 0, j)),
        out_shape=jax.ShapeDtypeStruct((depth, 8, n), F32),
        compiler_params=_cparams(("parallel", "parallel")),
        name="ada",
    )(cc, w_ada, b_ada.reshape(depth, 1, n))


def _inproj_kernel(x_ref, sh_ref, sc_ref, w_ref, wz_ref, o_ref, z_ref, h_scr):
    @pl.when(pl.program_id(2) == 0)
    def _():
        h = (_layer_norm_f32(x_ref[0]) * (1.0 + sc_ref[0]) + sh_ref[0]).astype(BF16)
        h_scr[...] = h
        z_ref[0] = jnp.dot(h, wz_ref[...], preferred_element_type=F32)

    o_ref[0] = jnp.dot(h_scr[...], w_ref[...], preferred_element_type=F32).astype(o_ref.dtype)


def _inproj(x, shift, scale, w, wz):
    bsz, length, d = x.shape
    n_cols = w.shape[1]
    tm = min(length, 1024)
    tn = 512
    return pl.pallas_call(
        _inproj_kernel,
        grid=(bsz, length // tm, n_cols // tn),
        in_specs=[
            pl.BlockSpec((1, tm, d), lambda b, i, j: (b, i, 0)),
            pl.BlockSpec((1, 1, d), lambda b, i, j: (b, 0, 0)),
            pl.BlockSpec((1, 1, d), lambda b, i, j: (b, 0, 0)),
            pl.BlockSpec((d, tn), lambda b, i, j: (0, j)),
            pl.BlockSpec((d, LANES), lambda b, i, j: (0, 0)),
        ],
        out_specs=[pl.BlockSpec((1, tm, tn), lambda b, i, j: (b, i, j)),
                   pl.BlockSpec((1, tm, LANES), lambda b, i, j: (b, i, 0))],
        out_shape=[jax.ShapeDtypeStruct((bsz, length, n_cols), BF16),
                   jax.ShapeDtypeStruct((bsz, length, LANES), F32)],
        scratch_shapes=[pltpu.VMEM((tm, d), BF16)],
        compiler_params=_cparams(("parallel", "parallel", "arbitrary")),
        name="inproj",
    )(x, shift, scale, w, wz)


def _rope_tables(length):
    n_freq = DA_HEAD_DIM // 4
    inv = ROPE_THETA ** (-jnp.arange(n_freq, dtype=F32) / n_freq)
    rows = jnp.repeat(jnp.arange(length // GRID_W), GRID_W).astype(F32)
    cols = (jnp.arange(length) % GRID_W).astype(F32)
    ang_r = rows[:, None] * inv
    ang_c = cols[:, None] * inv
    cos64 = jnp.concatenate([jnp.cos(ang_r), jnp.cos(ang_r), jnp.cos(ang_c), jnp.cos(ang_c)], axis=-1)
    sin64 = jnp.concatenate([-jnp.sin(ang_r), jnp.sin(ang_r), -jnp.sin(ang_c), jnp.sin(ang_c)], axis=-1)
    return jnp.concatenate([cos64, cos64], axis=-1), jnp.concatenate([sin64, sin64], axis=-1)


def _swap16(x):
    n = x.shape[-1]
    lane = lax.broadcasted_iota(jnp.int32, x.shape, x.ndim - 1)
    up = pltpu.roll(x, n - 16, axis=x.ndim - 1)
    dn = pltpu.roll(x, 16, axis=x.ndim - 1)
    return jnp.where((lane & 16) == 0, up, dn)


def _rope(x, cos, sin):
    return x * cos + _swap16(x) * sin


ATTN_TK = 512
ROPE_ROWS = 512


def _attn_kernel(*refs, n_src, rope, lam_init):
    lam_ref, g_ref, q_ref = refs[:3]
    pos = 3
    if rope:
        qcos_ref, qsin_ref, kcos_ref, ksin_ref = refs[pos:pos + 4]
        pos += 4
    kv_refs = refs[pos:pos + 2 * n_src]
    o_ref = refs[pos + 2 * n_src]
    scr = refs[pos + 2 * n_src + 1:]
    vx_scr = scr[:n_src]
    krot_scr = scr[n_src] if rope else None
    tq = q_ref.shape[1]

    @pl.when(pl.program_id(2) == 0)
    def _():
        for s_idx in range(n_src):
            v_ref = kv_refs[2 * s_idx + 1]
            n_keys = v_ref.shape[1]
            vx_scr[s_idx][:, 0:DA_V_DIM] = v_ref[0]
            vx_scr[s_idx][:, DA_V_DIM:2 * DA_V_DIM] = jnp.ones((n_keys, DA_V_DIM), BF16)
        if rope:
            k_ref = kv_refs[2 * (n_src - 1)]
            rr = min(ROPE_ROWS, k_ref.shape[1])

            def rot(j, carry):
                r0 = pl.multiple_of(j * rr, rr)
                kk = k_ref[0, pl.ds(r0, rr), :].astype(F32)
                krot_scr[pl.ds(r0, rr), :] = _rope(kk, kcos_ref[pl.ds(r0, rr), :],
                                                   ksin_ref[pl.ds(r0, rr), :]).astype(BF16)
                return carry

            lax.fori_loop(0, k_ref.shape[1] // rr, rot, 0)

    lp = lam_ref[...]
    lam = (jnp.exp(jnp.sum(lp[0] * lp[1], axis=-1, keepdims=True))
           - jnp.exp(jnp.sum(lp[2] * lp[3], axis=-1, keepdims=True)) + lam_init)

    q = q_ref[0].astype(F32)
    if rope:
        q = _rope(q, qcos_ref[...], qsin_ref[...])
    q = (q * (DA_HEAD_DIM ** -0.5 * math.log2(math.e))).astype(BF16)
    lane = lax.broadcasted_iota(jnp.int32, q.shape, 1)
    zero = jnp.zeros_like(q)
    qs = jnp.concatenate([jnp.where(lane < DA_HEAD_DIM, q, zero),
                          jnp.where(lane >= DA_HEAD_DIM, q, zero)], axis=0)

    m = jnp.full((2 * tq, 1), -1e30, F32)
    acc = jnp.zeros((2 * tq, 2 * DA_V_DIM), F32)
    for s_idx in range(n_src):
        k_ref = kv_refs[2 * s_idx]
        k_src = krot_scr if (rope and s_idx == n_src - 1) else k_ref.at[0]
        n_keys = k_ref.shape[1]
        tk = min(ATTN_TK, n_keys)
        for j in range(n_keys // tk):
            kb = k_src[j * tk:(j + 1) * tk, :]
            vb = vx_scr[s_idx][j * tk:(j + 1) * tk, :]
            s = lax.dot_general(qs, kb, (((1,), (1,)), ((), ())), preferred_element_type=F32)
            m_new = jnp.maximum(m, jnp.max(s, axis=-1, keepdims=True))
            p = jnp.exp2(s - m_new).astype(BF16)
            acc = jnp.exp2(m - m_new) * acc + jnp.dot(p, vb, preferred_element_type=F32)
            m = m_new

    o = acc[:, :DA_V_DIM] / acc[:, DA_V_DIM:]
    a = o[:tq] - lam * o[tq:]
    y = a * lax.rsqrt(jnp.mean(a * a, axis=-1, keepdims=True) + LN_EPS) * g_ref[...]
    o_ref[0] = (y * (1.0 - lam_init)).astype(o_ref.dtype)


def _diff_attention(q_arr, kv_arrs, lam_p, norm_g, lam_init, rope=None):
    bsz, lq, _ = q_arr.shape
    tq = min(lq, 256)
    in_specs = [
        pl.BlockSpec((4, 1, DA_HEAD_DIM), lambda b, h, i: (0, 0, 0)),
        pl.BlockSpec((1, DA_V_DIM), lambda b, h, i: (0, 0)),
        pl.BlockSpec((1, tq, LANES), lambda b, h, i: (b, i, OFF_AQ // LANES + h)),
    ]
    args = [lam_p.reshape(4, 1, DA_HEAD_DIM), norm_g.reshape(1, DA_V_DIM), q_arr]
    if rope is not None:
        cos, sin = rope
        in_specs += [pl.BlockSpec((tq, LANES), lambda b, h, i: (i, 0))] * 2
        in_specs += [pl.BlockSpec((lq, LANES), lambda b, h, i: (0, 0))] * 2
        args += [cos, sin, cos, sin]
    scratch = []
    for arr in kv_arrs:
        lk = arr.shape[1]
        in_specs += [pl.BlockSpec((1, lk, LANES), lambda b, h, i: (b, 0, OFF_AK // LANES + h)),
                     pl.BlockSpec((1, lk, LANES), lambda b, h, i: (b, 0, OFF_AV // LANES + h))]
        args += [arr, arr]
        scratch.append(pltpu.VMEM((lk, 2 * DA_V_DIM), BF16))
    if rope is not None:
        scratch.append(pltpu.VMEM((kv_arrs[-1].shape[1], LANES), BF16))
    return pl.pallas_call(
        functools.partial(_attn_kernel, n_src=len(kv_arrs), rope=rope is not None, lam_init=lam_init),
        grid=(bsz, DA_HEADS, lq // tq),
        in_specs=in_specs,
        out_specs=pl.BlockSpec((1, tq, LANES), lambda b, h, i: (b, i, h)),
        out_shape=jax.ShapeDtypeStruct((bsz, lq, DA_HEADS * DA_V_DIM), BF16),
        scratch_shapes=scratch,
        compiler_params=_cparams(("parallel", "parallel", "arbitrary")),
        name="diff_attn",
    )(*args)


def _gla_kernel(*refs, reverse, final):
    if final:
        (q_ref, k_ref, v_ref, z_ref, wg_ref, bg_ref, tri_ref, s0_ref, oprev_ref, r_ref, g_ref,
         o_ref, sfin_ref, s_scr) = refs
    else:
        (q_ref, k_ref, v_ref, z_ref, wg_ref, bg_ref, tri_ref, s0_ref,
         o_ref, sfin_ref, s_scr) = refs
    j = pl.program_id(1)
    tb = q_ref.shape[1]
    nch = tb // GLA_CHUNK
    c = GLA_CHUNK

    @pl.when(j == 0)
    def _():
        s_scr[...] = s0_ref[0]

    pre = jnp.dot(z_ref[0], wg_ref[...], precision=HIGHEST, preferred_element_type=F32) + bg_ref[...]
    la = (jnp.minimum(pre, 0.0) - jnp.log1p(jnp.exp(-jnp.abs(pre)))) * (1.0 / GLA_GATE_TAU)
    b_all = jnp.dot(tri_ref[...], la, precision=HIGHEST, preferred_element_type=F32)

    row = lax.broadcasted_iota(jnp.int32, (GLA_HEADS * c, c), 0) % c
    col = lax.broadcasted_iota(jnp.int32, (GLA_HEADS * c, c), 1)
    causal = (row <= col) if reverse else (row >= col)
    hrow = lax.broadcasted_iota(jnp.int32, (GLA_K_WIDTH, GLA_WIDTH), 0) // GLA_DK
    hcol = lax.broadcasted_iota(jnp.int32, (GLA_K_WIDTH, GLA_WIDTH), 1) // GLA_DV
    blockdiag = hrow == hcol
    qlane_head = lax.broadcasted_iota(jnp.int32, (c, GLA_K_WIDTH), 1) // GLA_DK
    ones_c = jnp.ones((c, LANES), F32)

    state = s_scr[...]
    order = range(nch - 1, -1, -1) if reverse else range(nch)
    for ci in order:
        r0 = ci * c
        bc = b_all[r0:r0 + c]
        lac = la[r0:r0 + c]
        if reverse:
            b_end, b_ref_row = bc[0:1], bc[c - 1 - c // 2:c - c // 2]
        else:
            b_end, b_ref_row = bc[c - 1:c], bc[c // 2:c // 2 + 1]
        qc = q_ref[0, r0:r0 + c, :].astype(F32) * (GLA_DK ** -0.5)
        kc = k_ref[0, r0:r0 + c, :].astype(F32)
        vc = v_ref[0, r0:r0 + c, :]
        qe = qc * jnp.exp(bc - b_ref_row)
        ke = (kc * jnp.exp(b_ref_row - bc)).astype(BF16)
        zq = jnp.zeros_like(qe)
        q_stack = jnp.concatenate([jnp.where(qlane_head == h, qe, zq) for h in range(GLA_HEADS)],
                                  axis=0).astype(BF16)
        att = lax.dot_general(q_stack, ke, (((1,), (1,)), ((), ())), preferred_element_type=F32)
        att = jnp.where(causal, att, 0.0).astype(BF16)
        o_intra = jnp.concatenate(
            [jnp.dot(att[h * c:(h + 1) * c], vc[:, h * GLA_DV:(h + 1) * GLA_DV], preferred_element_type=F32)
             for h in range(GLA_HEADS)], axis=-1)
        o_inter = jnp.dot((qc * jnp.exp(bc)).astype(BF16), state.astype(BF16), preferred_element_type=F32)
        o_chunk = o_intra + o_inter
        if final:
            o_chunk = o_chunk + oprev_ref[0, r0:r0 + c, :]
            parts = []
            for h in range(GLA_HEADS):
                oh = o_chunk[:, h * GLA_DV:(h + 1) * GLA_DV]
                parts.append(oh * lax.rsqrt(jnp.mean(oh * oh, axis=-1, keepdims=True) + LN_EPS) * g_ref[...])
            rr = r_ref[0, r0:r0 + c, :].astype(F32)
            o_ref[0, r0:r0 + c, :] = (jnp.concatenate(parts, axis=-1)
                                      * (rr * jax.nn.sigmoid(rr))).astype(o_ref.dtype)
        else:
            o_ref[0, r0:r0 + c, :] = o_chunk.astype(o_ref.dtype)
        kd_t = jnp.transpose(kc * jnp.exp(b_end - bc)).astype(BF16)
        upd = jnp.dot(kd_t, vc, preferred_element_type=F32)
        dec = jnp.exp(jnp.dot(jnp.transpose(lac), ones_c, precision=HIGHEST, preferred_element_type=F32))
        dec = jnp.concatenate([dec] * (GLA_WIDTH // LANES), axis=-1)
        state = dec * state + jnp.where(blockdiag, upd, 0.0)
    s_scr[...] = state
    sfin_ref[0] = state


def _gla_pass(pb, gz, wg_pad, bg, tri, s0, reverse, final_args=None):
    bsz, length, _ = pb.shape
    tb = min(length, 512)
    nb = length // tb
    blk = (lambda j: nb - 1 - j) if reverse else (lambda j: j)
    in_specs = [
        pl.BlockSpec((1, tb, GLA_K_WIDTH), lambda b, j: (b, blk(j), OFF_GQ // GLA_K_WIDTH)),
        pl.BlockSpec((1, tb, GLA_K_WIDTH), lambda b, j: (b, blk(j), OFF_GK // GLA_K_WIDTH)),
        pl.BlockSpec((1, tb, GLA_WIDTH), lambda b, j: (b, blk(j), OFF_GV // GLA_WIDTH)),
        pl.BlockSpec((1, tb, LANES), lambda b, j: (b, blk(j), 0)),
        pl.BlockSpec((LANES, GLA_K_WIDTH), lambda b, j: (0, 0)),
        pl.BlockSpec((1, GLA_K_WIDTH), lambda b, j: (0, 0)),
        pl.BlockSpec((tb, tb), lambda b, j: (0, 0)),
        pl.BlockSpec((1, GLA_K_WIDTH, GLA_WIDTH), lambda b, j: (b, 0, 0)),
    ]
    args = [pb, pb, pb, gz, wg_pad, bg, tri, s0]
    final = final_args is not None
    if final:
        o_prev, norm_g = final_args
        in_specs += [
            pl.BlockSpec((1, tb, GLA_WIDTH), lambda b, j: (b, blk(j), 0)),
            pl.BlockSpec((1, tb, GLA_WIDTH), lambda b, j: (b, blk(j), OFF_GR // GLA_WIDTH)),
            pl.BlockSpec((1, GLA_DV), lambda b, j: (0, 0)),
        ]
        args += [o_prev, pb, norm_g.reshape(1, GLA_DV)]
    return pl.pallas_call(
        functools.partial(_gla_kernel, reverse=reverse, final=final),
        grid=(bsz, nb),
        in_specs=in_specs,
        out_specs=[pl.BlockSpec((1, tb, GLA_WIDTH), lambda b, j: (b, blk(j), 0)),
                   pl.BlockSpec((1, GLA_K_WIDTH, GLA_WIDTH), lambda b, j: (b, 0, 0))],
        out_shape=[jax.ShapeDtypeStruct((bsz, length, GLA_WIDTH), BF16 if final else F32),
                   jax.ShapeDtypeStruct((bsz, GLA_K_WIDTH, GLA_WIDTH), F32)],
        scratch_shapes=[pltpu.VMEM((GLA_K_WIDTH, GLA_WIDTH), F32)],
        compiler_params=_cparams(("parallel", "arbitrary")),
        name="gla_bwd" if reverse else "gla_fwd",
    )(*args)


def _gla_consts(tb):
    tri = np.kron(np.eye(tb // GLA_CHUNK), np.tril(np.ones((GLA_CHUNK, GLA_CHUNK)))).astype(np.float32)
    return jnp.asarray(tri), jnp.asarray(tri.T.copy())


def _slab_rows(n_slabs):
    return n_slabs * SLAB_PITCH


def _store_slabs(ref, lead, val, n_slabs):
    rows, lanes = val.shape
    for s in range(n_slabs):
        base = s * SLAB_PITCH
        lo = s * FFT_N1
        if lo + FFT_N1 <= rows:
            ref[lead + (slice(base, base + FFT_N1), slice(None))] = val[lo:lo + FFT_N1]
            ref[lead + (slice(base + FFT_N1, base + SLAB_PITCH), slice(None))] = jnp.zeros(
                (SLAB_PITCH - FFT_N1, lanes), val.dtype)
        else:
            ref[lead + (slice(base, base + SLAB_PITCH), slice(None))] = jnp.zeros((SLAB_PITCH, lanes), val.dtype)


def _hy_filter_kernel(feat_ref, w1_ref, b1_ref, f1_ref, w2_ref, b2_ref, f2_ref, w3_ref, dl_ref, o_ref, *, length):
    tl = feat_ref.shape[0]
    feat = feat_ref[...]
    h = jnp.sin(f1_ref[...] * (jnp.dot(feat, w1_ref[...], precision=HIGHEST, preferred_element_type=F32)
                               + b1_ref[...]))
    h = jnp.sin(f2_ref[...] * (jnp.dot(h, w2_ref[...], precision=HIGHEST, preferred_element_type=F32)
                               + b2_ref[...]))
    h = jnp.dot(h, w3_ref[...], precision=HIGHEST, preferred_element_type=F32)
    win = jnp.exp(-feat[:, 0:1] * dl_ref[...])
    pos = lax.broadcasted_iota(jnp.int32, win.shape, 0) + pl.program_id(0) * tl
    valid = pos < length
    taps_f = jnp.where(valid, h[:, :HY_WIDTH] * win, 0.0)
    taps_b = jnp.where(jnp.logical_and(valid, pos > 0), h[:, HY_WIDTH:] * win, 0.0)
    _store_slabs(o_ref, (0,), taps_f, tl // FFT_N1)
    _store_slabs(o_ref, (1,), taps_b, tl // FFT_N1)


def _hy_feats(length, lpad):
    t = jnp.linspace(0.0, 1.0, length, dtype=F32)[:, None]
    bands = (HY_POS_DIM - 1) // 2
    w = 2.0 * math.pi * jnp.arange(length, dtype=F32)[:, None] / length
    f = jnp.linspace(1e-4, bands - 1, bands, dtype=F32)[None, :]
    feat = jnp.concatenate([t, jnp.cos(f * w), -jnp.sin(f * w)], axis=-1)
    return jnp.zeros((lpad, LANES), F32).at[:length, :HY_POS_DIM].set(feat)


def _hy_filters(feat, length, p):
    lpad = feat.shape[0]
    tl = min(lpad, 512)
    max_decay = math.log(HY_TARGET) / HY_FAST_PCT
    min_decay = math.log(HY_TARGET) / HY_SLOW_PCT
    deltas = jnp.abs(jnp.linspace(min_decay, max_decay, HY_WIDTH, dtype=F32))[None, :]
    w1 = jnp.zeros((LANES, HY_HIDDEN), F32).at[:HY_POS_DIM].set(p['hy_w1'])
    full = lambda shape: pl.BlockSpec(shape, lambda i: (0,) * len(shape))
    tl_slab = _slab_rows(tl // FFT_N1)
    return pl.pallas_call(
        functools.partial(_hy_filter_kernel, length=length),
        grid=(lpad // tl,),
        in_specs=[pl.BlockSpec((tl, LANES), lambda i: (i, 0)),
                  full((LANES, HY_HIDDEN)), full((1, HY_HIDDEN)), full((1, HY_HIDDEN)),
                  full((HY_HIDDEN, HY_HIDDEN)), full((1, HY_HIDDEN)), full((1, HY_HIDDEN)),
                  full((HY_HIDDEN, 2 * HY_WIDTH)), full((1, HY_WIDTH))],
        out_specs=pl.BlockSpec((2, tl_slab, HY_WIDTH), lambda i: (0, i, 0)),
        out_shape=jax.ShapeDtypeStruct((2, _slab_rows(lpad // FFT_N1), HY_WIDTH), F32),
        compiler_params=_cparams(("parallel",)),
        name="hy_filter",
    )(feat, w1, p['hy_b1'].reshape(1, -1), p['hy_freq1'].reshape(1, -1), p['hy_w2'],
      p['hy_b2'].reshape(1, -1), p['hy_freq2'].reshape(1, -1), p['hy_w3'], deltas)


def _hy_pre_kernel(x0_ref, x1_ref, vv_ref, w0_ref, w1_ref, wv_ref, b0_ref, b1_ref, bv_ref, x0o_ref, u_ref, *, n_slabs):
    length = x0_ref.shape[1]
    row = lax.broadcasted_iota(jnp.int32, (length, x0_ref.shape[2]), 0)

    def conv(x_ref, w_ref, b_ref):
        x = x_ref[0].astype(F32)
        w = w_ref[...]
        xm = jnp.where(row == 0, 0.0, pltpu.roll(x, 1, axis=0))
        xp = jnp.where(row == length - 1, 0.0, pltpu.roll(x, length - 1, axis=0))
        return xm * w[0:1] + x * w[1:2] + xp * w[2:3] + b_ref[...]

    x0 = conv(x0_ref, w0_ref, b0_ref)
    u = conv(x1_ref, w1_ref, b1_ref) * conv(vv_ref, wv_ref, bv_ref)
    _store_slabs(x0o_ref, (0,), x0, n_slabs)
    _store_slabs(u_ref, (0,), u, n_slabs)


def _hy_pre(pb, conv_w, conv_b, n_slabs):
    bsz, length, _ = pb.shape
    nct = HY_WIDTH // LANES
    base = OFF_HY // LANES
    xs = lambda g: pl.BlockSpec((1, length, LANES), lambda b, j, g=g: (b, 0, base + g * nct + j))
    ws = lambda g: pl.BlockSpec((3, LANES), lambda b, j, g=g: (0, g * nct + j))
    bs = lambda g: pl.BlockSpec((1, LANES), lambda b, j, g=g: (0, g * nct + j))
    cb = conv_b.reshape(1, -1)
    rows = _slab_rows(n_slabs)
    return pl.pallas_call(
        functools.partial(_hy_pre_kernel, n_slabs=n_slabs),
        grid=(bsz, nct),
        in_specs=[xs(0), xs(1), xs(2), ws(0), ws(1), ws(2), bs(0), bs(1), bs(2)],
        out_specs=[pl.BlockSpec((1, rows, LANES), lambda b, j: (b, 0, j))] * 2,
        out_shape=[jax.ShapeDtypeStruct((bsz, rows, HY_WIDTH), F32)] * 2,
        compiler_params=_cparams(("parallel", "parallel")),
        name="hy_pre",
    )(pb, pb, pb, conv_w, conv_w, conv_w, cb, cb, cb)


def _fft_consts(n2, nin):
    n1 = FFT_N1
    n = n1 * n2
    k2 = np.arange(n2)[:, None]
    m2 = np.arange(nin)[None, :]
    ang_a = 2.0 * np.pi * (k2 * m2 % n2) / n2
    fa = np.concatenate([np.cos(ang_a), -np.sin(ang_a)], axis=0)
    ga = np.concatenate([np.cos(ang_a).T, -np.sin(ang_a).T], axis=1) / n
    k1 = np.arange(n1)[None, :, None]
    j1 = np.arange(n1)[None, None, :]
    kk2 = np.arange(n2)[:, None, None]
    ang_b = 2.0 * np.pi * ((j1 * k1 * n2 + j1 * kk2) % n) / n
    mr, mi = np.cos(ang_b), -np.sin(ang_b)
    big = np.concatenate([np.concatenate([mr, -mi], axis=2), np.concatenate([mi, mr], axis=2)], axis=1)
    big_t = np.transpose(big, (0, 2, 1))
    f32 = lambda a: jnp.asarray(a.astype(np.float32))
    bf16 = lambda a: jnp.asarray(a.astype(np.float32)).astype(BF16)
    return f32(fa), f32(ga), bf16(big), bf16(big_t)


def _fft_a_kernel(f_ref, x_ref, o_ref):
    nin = f_ref.shape[1]
    n_out = f_ref.shape[0]
    groups = x_ref.shape[0]
    f = f_ref[...]

    def body(n1, carry):
        for g in range(groups):
            xs = x_ref[g, pl.ds(n1, nin, stride=SLAB_PITCH), :]
            o_ref[g, pl.ds(n1, n_out, stride=SLAB_PITCH), :] = jnp.dot(f, xs, preferred_element_type=F32)
        return carry

    lax.fori_loop(0, FFT_N1, body, 0, unroll=8)
    zeros = jnp.zeros((n_out, x_ref.shape[2]), F32)
    for g in range(groups):
        for r in range(FFT_N1, SLAB_PITCH):
            o_ref[g, pl.ds(r, n_out, stride=SLAB_PITCH), :] = zeros


def _fft_stage_a(x, fa):
    g, rows, ch = x.shape
    n_out, nin = fa.shape
    assert rows == _slab_rows(nin)
    return pl.pallas_call(
        _fft_a_kernel,
        grid=(g, ch // LANES),
        in_specs=[pl.BlockSpec(fa.shape, lambda b, j: (0, 0)),
                  pl.BlockSpec((1, rows, LANES), lambda b, j: (b, 0, j))],
        out_specs=pl.BlockSpec((1, _slab_rows(n_out), LANES), lambda b, j: (b, 0, j)),
        out_shape=jax.ShapeDtypeStruct((g, _slab_rows(n_out), ch), F32),
        compiler_params=_cparams(("parallel", "parallel")),
        name="fft_stage_a",
    )(fa, x)


def _fft_spec_kernel(m_ref, a_ref, o_ref):
    big = m_ref[0]
    xf = jnp.dot(big, jnp.concatenate([a_ref[0, 0, 0], a_ref[0, 1, 0]], axis=0).astype(BF16),
                 preferred_element_type=F32)
    xb = jnp.dot(big, jnp.concatenate([a_ref[1, 0, 0], a_ref[1, 1, 0]], axis=0).astype(BF16),
                 preferred_element_type=F32)
    o_ref[0, 0] = xf[:FFT_N1] + xb[:FFT_N1]
    o_ref[1, 0] = xf[FFT_N1:] - xb[FFT_N1:]


def _filter_spectrum(a_taps, big, n2):
    ch = a_taps.shape[-1]
    av = a_taps.reshape(2, 2, n2, SLAB_PITCH, ch)
    return pl.pallas_call(
        _fft_spec_kernel,
        grid=(n2,),
        in_specs=[pl.BlockSpec((1, 2 * FFT_N1, 2 * FFT_N1), lambda k: (k, 0, 0)),
                  pl.BlockSpec((2, 2, 1, FFT_N1, ch), lambda k: (0, 0, k, 0, 0))],
        out_specs=pl.BlockSpec((2, 1, FFT_N1, ch), lambda k: (0, k, 0, 0)),
        out_shape=jax.ShapeDtypeStruct((2, n2, FFT_N1, ch), F32),
        compiler_params=_cparams(("parallel",)),
        name="fft_filter_spec",
    )(big, av)


def _fft_conv_kernel(m_ref, mt_ref, h_ref, a_ref, o_ref):
    x = jnp.dot(m_ref[0], jnp.concatenate([a_ref[0, 0, 0], a_ref[0, 1, 0]], axis=0).astype(BF16),
                preferred_element_type=F32)
    xr, xi = x[:FFT_N1], x[FFT_N1:]
    hr, hi = h_ref[0, 0], h_ref[1, 0]
    y = jnp.concatenate([xr * hr - xi * hi, xr * hi + xi * hr], axis=0).astype(BF16)
    t = jnp.dot(mt_ref[0], y, preferred_element_type=F32)
    o_ref[0, 0, 0] = t[:FFT_N1]
    o_ref[0, 1, 0] = t[FFT_N1:]


def _fft_conv_mid(a_sig, spec, big, big_t, n2):
    bsz = a_sig.shape[0]
    ch = a_sig.shape[-1]
    av = a_sig.reshape(bsz, 2, n2, SLAB_PITCH, ch)
    out = pl.pallas_call(
        _fft_conv_kernel,
        grid=(n2, bsz),
        in_specs=[pl.BlockSpec((1, 2 * FFT_N1, 2 * FFT_N1), lambda k, b: (k, 0, 0)),
                  pl.BlockSpec((1, 2 * FFT_N1, 2 * FFT_N1), lambda k, b: (k, 0, 0)),
                  pl.BlockSpec((2, 1, FFT_N1, ch), lambda k, b: (0, k, 0, 0)),
                  pl.BlockSpec((1, 2, 1, FFT_N1, ch), lambda k, b: (b, 0, k, 0, 0))],
        out_specs=pl.BlockSpec((1, 2, 1, FFT_N1, ch), lambda k, b: (b, 0, k, 0, 0)),
        out_shape=jax.ShapeDtypeStruct((bsz, 2, n2, SLAB_PITCH, ch), F32),
        compiler_params=_cparams(("parallel", "parallel")),
        name="fft_conv_mid",
    )(big, big_t, spec, av)
    return out.reshape(bsz, 2 * n2 * SLAB_PITCH, ch)


def _fft_out_kernel(g_ref, t_ref, x0_ref, u_ref, skip_ref, o_ref):
    nin, n_in = g_ref.shape
    g = g_ref[...]
    skip = skip_ref[...]

    def body(n1, carry):
        ts = t_ref[0, pl.ds(n1, n_in, stride=SLAB_PITCH), :]
        y = jnp.dot(g, ts, preferred_element_type=F32)
        x0 = x0_ref[0, pl.ds(n1, nin, stride=SLAB_PITCH), :]
        u = u_ref[0, pl.ds(n1, nin, stride=SLAB_PITCH), :]
        o_ref[0, pl.ds(n1, nin, stride=FFT_N1), :] = x0 * (y + skip * u)
        return carry

    lax.fori_loop(0, FFT_N1, body, 0, unroll=8)


def _fft_out(t, ga, x0, u, skip):
    bsz, rows, ch = u.shape
    nin, n_in = ga.shape
    return pl.pallas_call(
        _fft_out_kernel,
        grid=(bsz, ch // LANES),
        in_specs=[pl.BlockSpec(ga.shape, lambda b, j: (0, 0)),
                  pl.BlockSpec((1, _slab_rows(n_in), LANES), lambda b, j: (b, 0, j)),
                  pl.BlockSpec((1, rows, LANES), lambda b, j: (b, 0, j)),
                  pl.BlockSpec((1, rows, LANES), lambda b, j: (b, 0, j)),
                  pl.BlockSpec((1, LANES), lambda b, j: (0, j))],
        out_specs=pl.BlockSpec((1, nin * FFT_N1, LANES), lambda b, j: (b, 0, j)),
        out_shape=jax.ShapeDtypeStruct((bsz, nin * FFT_N1, ch), F32),
        compiler_params=_cparams(("parallel", "parallel")),
        name="fft_out",
    )(ga, t, x0, u, skip.reshape(1, ch))


def _hyena(pb, p, consts, length):
    fa, ga, big, big_t, feat, n2, nin = consts
    taps = _hy_filters(feat, length, p)
    spec = _filter_spectrum(_fft_stage_a(taps, fa), big, n2)
    x0, u = _hy_pre(pb, p['hy_conv_w'], p['hy_conv_b'], nin)
    t = _fft_conv_mid(_fft_stage_a(u, fa), spec, big, big_t, n2)
    y = _fft_out(t, ga, x0, u, p['hy_skip'])
    return y if y.shape[1] == length else y[:, :length]


def _hyena_consts(length):
    if length >= 2048:
        lpad, n2 = length, 2 * length // FFT_N1
    else:
        lpad = max(4 * length, 1024)
        n2 = lpad // FFT_N1
    nin = lpad // FFT_N1
    return _fft_consts(n2, nin) + (_hy_feats(length, lpad), n2, nin)


def _merge_kernel(ya_ref, yb_ref, yc_ref, lg_ref, x_ref, gate_ref, lng_ref, lnb_ref, wb_ref, wo_ref, o_ref):
    lg = lg_ref[0].astype(F32)
    ys = (ya_ref[0], yb_ref[0], yc_ref[0].astype(BF16))
    mix = None
    for g in range(N_BRANCH):
        proj = jnp.dot(ys[g], wb_ref[g], preferred_element_type=F32)
        term = jax.nn.sigmoid(lg[:, g * D_MODEL:(g + 1) * D_MODEL]) * proj
        mix = term if mix is None else mix + term
    y = jnp.dot(mix.astype(BF16), wo_ref[...], preferred_element_type=F32)
    z = DEEPNORM_ALPHA * x_ref[0] + gate_ref[0] * y
    o_ref[0] = _layer_norm_f32(z) * lng_ref[...] + lnb_ref[...]


def _merge(ya, yb, yc, pb, x, gate, ln_g, ln_b, w_branch, w_out):
    bsz, length, d = x.shape
    tm = min(length, 512)
    return pl.pallas_call(
        _merge_kernel,
        grid=(bsz, length // tm),
        in_specs=[pl.BlockSpec((1, tm, 512), lambda b, i: (b, i, 0)),
                  pl.BlockSpec((1, tm, 512), lambda b, i: (b, i, 0)),
                  pl.BlockSpec((1, tm, 512), lambda b, i: (b, i, 0)),
                  pl.BlockSpec((1, tm, N_BRANCH * d), lambda b, i: (b, i, OFF_GATE // (N_BRANCH * d))),
                  pl.BlockSpec((1, tm, d), lambda b, i: (b, i, 0)),
                  pl.BlockSpec((1, 1, d), lambda b, i: (b, 0, 0)),
                  pl.BlockSpec((1, d), lambda b, i: (0, 0)),
                  pl.BlockSpec((1, d), lambda b, i: (0, 0)),
                  pl.BlockSpec((N_BRANCH, 512, d), lambda b, i: (0, 0, 0)),
                  pl.BlockSpec((d, d), lambda b, i: (0, 0))],
        out_specs=pl.BlockSpec((1, tm, d), lambda b, i: (b, i, 0)),
        out_shape=jax.ShapeDtypeStruct((bsz, length, d), F32),
        compiler_params=_cparams(("parallel", "parallel")),
        name="merge",
    )(ya, yb, yc, pb, x, gate, ln_g.reshape(1, d), ln_b.reshape(1, d), w_branch, w_out)


FF_CHUNK = 256


def _gelu_tanh(x):
    return 0.5 * x * (1.0 + jnp.tanh(math.sqrt(2.0 / math.pi) * (x + 0.044715 * (x * x * x))))


def _ffn_kernel(xp_ref, x_ref, xn_ref, sh_ref, sc_ref, gate_ref, lng_ref, lnb_ref,
                wup_ref, cw_ref, cb_ref, wdn_ref, o_ref, h_scr, acc_scr):
    i = pl.program_id(1)
    nt = pl.num_programs(1)
    tm = x_ref.shape[1]
    sc, sh = 1.0 + sc_ref[0], sh_ref[0]
    x = x_ref[0]
    h_scr[0:8, :] = _layer_norm_f32(xp_ref[0]) * sc + sh
    h_scr[8:8 + tm, :] = _layer_norm_f32(x) * sc + sh
    h_scr[8 + tm:16 + tm, :] = _layer_norm_f32(xn_ref[0]) * sc + sh
    hext = h_scr[...].astype(BF16)

    row = lax.broadcasted_iota(jnp.int32, (tm, FF_CHUNK), 0)
    keep_prev = jnp.logical_or(row > 0, i > 0)
    keep_next = jnp.logical_or(row < tm - 1, i < nt - 1)

    def conv(up, w, b):
        um = jnp.where(keep_prev, up[7:7 + tm], 0.0)
        un = jnp.where(keep_next, up[9:9 + tm], 0.0)
        return um * w[0:1] + up[8:8 + tm] * w[1:2] + un * w[2:3] + b

    n_chunks = D_FF // FF_CHUNK
    for cidx in range(n_chunks):
        ca = cidx * FF_CHUNK
        cg = D_FF + cidx * FF_CHUNK
        up_a = jnp.dot(hext, wup_ref[:, ca:ca + FF_CHUNK], preferred_element_type=F32)
        up_g = jnp.dot(hext, wup_ref[:, cg:cg + FF_CHUNK], preferred_element_type=F32)
        a = conv(up_a, cw_ref[:, ca:ca + FF_CHUNK], cb_ref[:, ca:ca + FF_CHUNK])
        g = conv(up_g, cw_ref[:, cg:cg + FF_CHUNK], cb_ref[:, cg:cg + FF_CHUNK])
        act = (_gelu_tanh(g) * a).astype(BF16)
        part = jnp.dot(act, wdn_ref[ca:ca + FF_CHUNK, :], preferred_element_type=F32)
        if cidx == 0:
            acc_scr[...] = part
        else:
            acc_scr[...] += part
    z = DEEPNORM_ALPHA * x + gate_ref[0] * acc_scr[...]
    o_ref[0] = _layer_norm_f32(z) * lng_ref[...] + lnb_ref[...]


def _ffn(x, shift, scale, gate, ln_g, ln_b, w_up, conv_w, conv_b, w_down):
    bsz, length, d = x.shape
    tm = min(length, 512)
    nt = length // tm
    r8 = tm // 8
    last8 = length // 8 - 1
    vec = pl.BlockSpec((1, 1, d), lambda b, i: (b, 0, 0))
    row = pl.BlockSpec((1, d), lambda b, i: (0, 0))
    return pl.pallas_call(
        _ffn_kernel,
        grid=(bsz, nt),
        in_specs=[pl.BlockSpec((1, 8, d), lambda b, i: (b, jnp.maximum(i * r8 - 1, 0), 0)),
                  pl.BlockSpec((1, tm, d), lambda b, i: (b, i, 0)),
                  pl.BlockSpec((1, 8, d), lambda b, i: (b, jnp.minimum((i + 1) * r8, last8), 0)),
                  vec, vec, vec, row, row,
                  pl.BlockSpec((d, 2 * D_FF), lambda b, i: (0, 0)),
                  pl.BlockSpec((3, 2 * D_FF), lambda b, i: (0, 0)),
                  pl.BlockSpec((1, 2 * D_FF), lambda b, i: (0, 0)),
                  pl.BlockSpec((D_FF, d), lambda b, i: (0, 0))],
        out_specs=pl.BlockSpec((1, tm, d), lambda b, i: (b, i, 0)),
        out_shape=jax.ShapeDtypeStruct((bsz, length, d), F32),
        scratch_shapes=[pltpu.VMEM((tm + 16, d), F32), pltpu.VMEM((tm, d), F32)],
        compiler_params=_cparams(("parallel", "parallel")),
        name="conv_ffn",
    )(x, x, x, shift, scale, gate, ln_g.reshape(1, d), ln_b.reshape(1, d),
      w_up, conv_w, conv_b.reshape(1, -1), w_down)


def _prep_w_in(w_in):
    a_q, a_k, a_v, g_q, g_k, g_v, g_r, g_z, hy, gate = jnp.split(
        w_in, [512, 1024, 1536, 1792, 2048, 2560, 3072, 3104, 4640], axis=-1)
    wb = jnp.concatenate([a_q, a_k, a_v, g_q, g_k, g_v, g_r, gate, hy], axis=-1).astype(BF16)
    wz = jnp.concatenate([g_z, jnp.zeros((w_in.shape[0], LANES - 2 * GLA_GATE_RANK), w_in.dtype)],
                         axis=-1).astype(BF16)
    return wb, wz


def _prep_gate(w_gate, b_gate):
    pads = []
    for d in range(2):
        m = jnp.zeros((LANES, GLA_K_WIDTH), F32).at[d * GLA_GATE_RANK:(d + 1) * GLA_GATE_RANK].set(w_gate[d])
        pads.append(m)
    return pads, [b_gate[0].reshape(1, -1), b_gate[1].reshape(1, -1)]


def kernel(x, c, ctx, c_ctx, w_ada, b_ada, w_in, da_lambda, da_norm_g, gla_w_gate, gla_b_gate, gla_norm_g,
           hy_conv_w, hy_conv_b, hy_w1, hy_b1, hy_freq1, hy_w2, hy_b2, hy_freq2, hy_w3, hy_skip, w_branch,
           w_out, ln1_g, ln1_b, ffn_w_up, ffn_conv_w, ffn_conv_b, ffn_w_down, ln2_g, ln2_b):
    bsz, length, d = x.shape
    c_len = ctx.shape[1]
    depth = w_ada.shape[0]

    cc = jnp.zeros((8, d), F32).at[:bsz].set(c).at[bsz].set(c_ctx)
    mod = _ada_all(cc, w_ada, b_ada)

    rope = _rope_tables(length)
    hy_lat = _hyena_consts(length)
    hy_ctx = _hyena_consts(c_len)
    tri_lat = _gla_consts(min(length, 512))
    tri_ctx = _gla_consts(min(c_len, 512))
    zero_state = jnp.zeros((bsz, GLA_K_WIDTH, GLA_WIDTH), F32)

    x_lat, x_ctx = x, ctx
    for l in range(depth):
        with_ctx = l < depth - 1
        lam_init = 0.8 - 0.6 * math.exp(-0.3 * l)
        p = {'hy_conv_w': hy_conv_w[l], 'hy_conv_b': hy_conv_b[l], 'hy_w1': hy_w1[l], 'hy_b1': hy_b1[l],
             'hy_freq1': hy_freq1[l], 'hy_w2': hy_w2[l], 'hy_b2': hy_b2[l], 'hy_freq2': hy_freq2[l],
             'hy_w3': hy_w3[l], 'hy_skip': hy_skip[l]}
        wb, wz = _prep_w_in(w_in[l])
        wg_pad, bg = _prep_gate(gla_w_gate[l], gla_b_gate[l])
        wbr = w_branch[l].astype(BF16)
        wo = w_out[l].astype(BF16)
        wup = ffn_w_up[l].astype(BF16)
        wdn = ffn_w_down[l].astype(BF16)

        m_lat = mod[l, :bsz].reshape(bsz, 1, 6 * d)
        m_ctx = jnp.broadcast_to(mod[l, bsz].reshape(1, 1, 6 * d), (bsz, 1, 6 * d))
        sh1, sc1, g1, sh2, sc2, g2 = [m_lat[..., k * d:(k + 1) * d] for k in range(6)]
        csh1, csc1, cg1, csh2, csc2, cg2 = [m_ctx[..., k * d:(k + 1) * d] for k in range(6)]

        pb, gz = _inproj(x_lat, sh1, sc1, wb, wz)
        cpb, cgz = _inproj(x_ctx, csh1, csc1, wb, wz)

        y_a = _diff_attention(pb, [cpb, pb], da_lambda[l], da_norm_g[l], lam_init, rope=rope)
        co_f, cs_f = _gla_pass(cpb, cgz, wg_pad[0], bg[0], tri_ctx[0], zero_state, False)
        if with_ctx:
            y_cb, cs_b = _gla_pass(cpb, cgz, wg_pad[1], bg[1], tri_ctx[1], zero_state, True,
                                   final_args=(co_f, gla_norm_g[l]))
        else:
            _, cs_b = _gla_pass(cpb, cgz, wg_pad[1], bg[1], tri_ctx[1], zero_state, True)
        o_f, _ = _gla_pass(pb, gz, wg_pad[0], bg[0], tri_lat[0], cs_f, False)
        y_b, _ = _gla_pass(pb, gz, wg_pad[1], bg[1], tri_lat[1], cs_b, True, final_args=(o_f, gla_norm_g[l]))
        y_c = _hyena(pb, p, hy_lat, length)

        x_lat = _merge(y_a, y_b, y_c, pb, x_lat, g1, ln1_g[l], ln1_b[l], wbr, wo)
        x_lat = _ffn(x_lat, sh2, sc2, g2, ln2_g[l], ln2_b[l], wup, ffn_conv_w[l], ffn_conv_b[l], wdn)
        if with_ctx:
            y_ca = _diff_attention(cpb, [cpb], da_lambda[l], da_norm_g[l], lam_init)
            y_cc = _hyena(cpb, p, hy_ctx, c_len)
            x_ctx = _merge(y_ca, y_cb, y_cc, cpb, x_ctx, cg1, ln1_g[l], ln1_b[l], wbr, wo)
            x_ctx = _ffn(x_ctx, csh2, csc2, cg2, ln2_g[l], ln2_b[l], wup, ffn_conv_w[l], ffn_conv_b[l], wdn)
    return x_lat
```

```python
import functools
import math

import numpy as np
import jax
import jax.numpy as jnp
from jax import lax
from jax.experimental import pallas as pl
from jax.experimental.pallas import tpu as pltpu

F32 = jnp.float32
BF16 = jnp.bfloat16
HIGHEST = lax.Precision.HIGHEST

D_MODEL = 1024
DEPTH = 4
GRID_W = 64
DA_HEADS = 4
DA_HEAD_DIM = 64
DA_V_DIM = 128
ROPE_THETA = 10000.0
GLA_HEADS = 4
GLA_DK = 64
GLA_DV = 128
GLA_K_WIDTH = 256
GLA_WIDTH = 512
GLA_GATE_RANK = 16
GLA_GATE_TAU = 16.0
GLA_CHUNK = 64
HY_WIDTH = 512
HY_POS_DIM = 33
HY_HIDDEN = 64
HY_TARGET = 1e-2
HY_FAST_PCT = 0.3
HY_SLOW_PCT = 1.5
N_BRANCH = 3
D_FF = 2816
LN_EPS = 1e-5
DEEPNORM_ALPHA = (2.0 * DEPTH) ** 0.25

NB_COLS = 7680
OFF_AQ, OFF_AK, OFF_AV, OFF_GQ, OFF_GK, OFF_GV, OFF_GR, OFF_GATE, OFF_HY = (
    0, 512, 1024, 1536, 1792, 2048, 2560, 3072, 6144)

LANES = 128
FFT_N1 = 128
SLAB_PITCH = 136
VMEM_LIMIT = 56 * 1024 * 1024


def _cparams(sem):
    return pltpu.CompilerParams(dimension_semantics=sem, vmem_limit_bytes=VMEM_LIMIT)


def _layer_norm_f32(x):
    mu = jnp.mean(x, axis=-1, keepdims=True)
    xc = x - mu
    var = jnp.mean(xc * xc, axis=-1, keepdims=True)
    return xc * lax.rsqrt(var + LN_EPS)


def _ada_kernel(c_ref, w_ref, b_ref, o_ref):
    cc = c_ref[...]
    s = cc * jax.nn.sigmoid(cc)
    o_ref[0] = jnp.dot(s, w_ref[0], precision=HIGHEST, preferred_element_type=F32) + b_ref[0]


def _ada_all(cc, w_ada, b_ada):
    depth, d, n = w_ada.shape
    tn = 1536
    return pl.pallas_call(
        _ada_kernel,
        grid=(depth, n // tn),
        in_specs=[
            pl.BlockSpec((8, d), lambda l, j: (0, 0)),
            pl.BlockSpec((1, d, tn), lambda l, j: (l, 0, j)),
            pl.BlockSpec((1, 1, tn), lambda l, j: (l, 0, j)),
        ],
        out_specs=pl.BlockSpec((1, 8, tn), lambda l, j: (l, 0, j)),
        out_shape=jax.ShapeDtypeStruct((depth, 8, n), F32),
        compiler_params=_cparams(("parallel", "parallel")),
        name="ada",
    )(cc, w_ada, b_ada.reshape(depth, 1, n))


def _inproj_kernel(x_ref, sh_ref, sc_ref, w_ref, wz_ref, o_ref, z_ref, h_scr):
    @pl.when(pl.program_id(2) == 0)
    def _():
        h = (_layer_norm_f32(x_ref[0]) * (1.0 + sc_ref[0]) + sh_ref[0]).astype(BF16)
        h_scr[...] = h
        z_ref[0] = jnp.dot(h, wz_ref[...], preferred_element_type=F32)

    o_ref[0] = jnp.dot(h_scr[...], w_ref[...], preferred_element_type=F32).astype(o_ref.dtype)


def _inproj(x, shift, scale, w, wz):
    bsz, length, d = x.shape
    n_cols = w.shape[1]
    tm = min(length, 1024)
    tn = 512
    return pl.pallas_call(
        _inproj_kernel,
        grid=(bsz, length // tm, n_cols // tn),
        in_specs=[
            pl.BlockSpec((1, tm, d), lambda b, i, j: (b, i, 0)),
            pl.BlockSpec((1, 1, d), lambda b, i, j: (b, 0, 0)),
            pl.BlockSpec((1, 1, d), lambda b, i, j: (b, 0, 0)),
            pl.BlockSpec((d, tn), lambda b, i, j: (0, j)),
            pl.BlockSpec((d, LANES), lambda b, i, j: (0, 0)),
        ],
        out_specs=[pl.BlockSpec((1, tm, tn), lambda b, i, j: (b, i, j)),
                   pl.BlockSpec((1, tm, LANES), lambda b, i, j: (b, i, 0))],
        out_shape=[jax.ShapeDtypeStruct((bsz, length, n_cols), BF16),
                   jax.ShapeDtypeStruct((bsz, length, LANES), F32)],
        scratch_shapes=[pltpu.VMEM((tm, d), BF16)],
        compiler_params=_cparams(("parallel", "parallel", "arbitrary")),
        name="inproj",
    )(x, shift, scale, w, wz)


def _rope_tables(length):
    n_freq = DA_HEAD_DIM // 4
    inv = ROPE_THETA ** (-jnp.arange(n_freq, dtype=F32) / n_freq)
    rows = jnp.repeat(jnp.arange(length // GRID_W), GRID_W).astype(F32)
    cols = (jnp.arange(length) % GRID_W).astype(F32)
    ang_r = rows[:, None] * inv
    ang_c = cols[:, None] * inv
    cos64 = jnp.concatenate([jnp.cos(ang_r), jnp.cos(ang_r), jnp.cos(ang_c), jnp.cos(ang_c)], axis=-1)
    sin64 = jnp.concatenate([-jnp.sin(ang_r), jnp.sin(ang_r), -jnp.sin(ang_c), jnp.sin(ang_c)], axis=-1)
    return jnp.concatenate([cos64, cos64], axis=-1), jnp.concatenate([sin64, sin64], axis=-1)


def _swap16(x):
    n = x.shape[-1]
    lane = lax.broadcasted_iota(jnp.int32, x.shape, x.ndim - 1)
    up = pltpu.roll(x, n - 16, axis=x.ndim - 1)
    dn = pltpu.roll(x, 16, axis=x.ndim - 1)
    return jnp.where((lane & 16) == 0, up, dn)


def _rope(x, cos, sin):
    return x * cos + _swap16(x) * sin


ATTN_TK = 512
ROPE_ROWS = 512


def _attn_kernel(*refs, n_src, rope, lam_init):
    lam_ref, g_ref, q_ref = refs[:3]
    pos = 3
    if rope:
        qcos_ref, qsin_ref, kcos_ref, ksin_ref = refs[pos:pos + 4]
        pos += 4
    kv_refs = refs[pos:pos + 2 * n_src]
    o_ref = refs[pos + 2 * n_src]
    scr = refs[pos + 2 * n_src + 1:]
    vx_scr = scr[:n_src]
    krot_scr = scr[n_src] if rope else None
    tq = q_ref.shape[1]

    @pl.when(pl.program_id(2) == 0)
    def _():
        for s_idx in range(n_src):
            v_ref = kv_refs[2 * s_idx + 1]
            n_keys = v_ref.shape[1]
            vx_scr[s_idx][:, 0:DA_V_DIM] = v_ref[0]
            vx_scr[s_idx][:, DA_V_DIM:2 * DA_V_DIM] = jnp.ones((n_keys, DA_V_DIM), BF16)
        if rope:
            k_ref = kv_refs[2 * (n_src - 1)]
            rr = min(ROPE_ROWS, k_ref.shape[1])

            def rot(j, carry):
                r0 = pl.multiple_of(j * rr, rr)
                kk = k_ref[0, pl.ds(r0, rr), :].astype(F32)
                krot_scr[pl.ds(r0, rr), :] = _rope(kk, kcos_ref[pl.ds(r0, rr), :],
                                                   ksin_ref[pl.ds(r0, rr), :]).astype(BF16)
                return carry

            lax.fori_loop(0, k_ref.shape[1] // rr, rot, 0)

    lp = lam_ref[...]
    lam = (jnp.exp(jnp.sum(lp[0] * lp[1], axis=-1, keepdims=True))
           - jnp.exp(jnp.sum(lp[2] * lp[3], axis=-1, keepdims=True)) + lam_init)

    q = q_ref[0].astype(F32)
    if rope:
        q = _rope(q, qcos_ref[...], qsin_ref[...])
    q = (q * (DA_HEAD_DIM ** -0.5 * math.log2(math.e))).astype(BF16)
    lane = lax.broadcasted_iota(jnp.int32, q.shape, 1)
    zero = jnp.zeros_like(q)
    qs = jnp.concatenate([jnp.where(lane < DA_HEAD_DIM, q, zero),
                          jnp.where(lane >= DA_HEAD_DIM, q, zero)], axis=0)

    m = jnp.full((2 * tq, 1), -1e30, F32)
    acc = jnp.zeros((2 * tq, 2 * DA_V_DIM), F32)
    for s_idx in range(n_src):
        k_ref = kv_refs[2 * s_idx]
        k_src = krot_scr if (rope and s_idx == n_src - 1) else k_ref.at[0]
        n_keys = k_ref.shape[1]
        tk = min(ATTN_TK, n_keys)
        for j in range(n_keys // tk):
            kb = k_src[j * tk:(j + 1) * tk, :]
            vb = vx_scr[s_idx][j * tk:(j + 1) * tk, :]
            s = lax.dot_general(qs, kb, (((1,), (1,)), ((), ())), preferred_element_type=F32)
            m_new = jnp.maximum(m, jnp.max(s, axis=-1, keepdims=True))
            p = jnp.exp2(s - m_new).astype(BF16)
            acc = jnp.exp2(m - m_new) * acc + jnp.dot(p, vb, preferred_element_type=F32)
            m = m_new

    o = acc[:, :DA_V_DIM] / acc[:, DA_V_DIM:]
    a = o[:tq] - lam * o[tq:]
    y = a * lax.rsqrt(jnp.mean(a * a, axis=-1, keepdims=True) + LN_EPS) * g_ref[...]
    o_ref[0] = (y * (1.0 - lam_init)).astype(o_ref.dtype)


def _diff_attention(q_arr, kv_arrs, lam_p, norm_g, lam_init, rope=None):
    bsz, lq, _ = q_arr.shape
    tq = min(lq, 256)
    in_specs = [
        pl.BlockSpec((4, 1, DA_HEAD_DIM), lambda b, h, i: (0, 0, 0)),
        pl.BlockSpec((1, DA_V_DIM), lambda b, h, i: (0, 0)),
        pl.BlockSpec((1, tq, LANES), lambda b, h, i: (b, i, OFF_AQ // LANES + h)),
    ]
    args = [lam_p.reshape(4, 1, DA_HEAD_DIM), norm_g.reshape(1, DA_V_DIM), q_arr]
    if rope is not None:
        cos, sin = rope
        in_specs += [pl.BlockSpec((tq, LANES), lambda b, h, i: (i, 0))] * 2
        in_specs += [pl.BlockSpec((lq, LANES), lambda b, h, i: (0, 0))] * 2
        args += [cos, sin, cos, sin]
    scratch = []
    for arr in kv_arrs:
        lk = arr.shape[1]
        in_specs += [pl.BlockSpec((1, lk, LANES), lambda b, h, i: (b, 0, OFF_AK // LANES + h)),
                     pl.BlockSpec((1, lk, LANES), lambda b, h, i: (b, 0, OFF_AV // LANES + h))]
        args += [arr, arr]
        scratch.append(pltpu.VMEM((lk, 2 * DA_V_DIM), BF16))
    if rope is not None:
        scratch.append(pltpu.VMEM((kv_arrs[-1].shape[1], LANES), BF16))
    return pl.pallas_call(
        functools.partial(_attn_kernel, n_src=len(kv_arrs), rope=rope is not None, lam_init=lam_init),
        grid=(bsz, DA_HEADS, lq // tq),
        in_specs=in_specs,
        out_specs=pl.BlockSpec((1, tq, LANES), lambda b, h, i: (b, i, h)),
        out_shape=jax.ShapeDtypeStruct((bsz, lq, DA_HEADS * DA_V_DIM), BF16),
        scratch_shapes=scratch,
        compiler_params=_cparams(("parallel", "parallel", "arbitrary")),
        name="diff_attn",
    )(*args)


def _gla_kernel(*refs, reverse, final):
    if final:
        (q_ref, k_ref, v_ref, z_ref, wg_ref, bg_ref, tri_ref, s0_ref, oprev_ref, r_ref, g_ref,
         o_ref, sfin_ref, s_scr) = refs
    else:
        (q_ref, k_ref, v_ref, z_ref, wg_ref, bg_ref, tri_ref, s0_ref,
         o_ref, sfin_ref, s_scr) = refs
    j = pl.program_id(1)
    tb = q_ref.shape[1]
    nch = tb // GLA_CHUNK
    c = GLA_CHUNK

    @pl.when(j == 0)
    def _():
        s_scr[...] = s0_ref[0]

    pre = jnp.dot(z_ref[0], wg_ref[...], precision=HIGHEST, preferred_element_type=F32) + bg_ref[...]
    la = (jnp.minimum(pre, 0.0) - jnp.log1p(jnp.exp(-jnp.abs(pre)))) * (1.0 / GLA_GATE_TAU)
    la_hi = la.astype(BF16)
    la_lo = (la - la_hi.astype(F32)).astype(BF16)
    bsum = jnp.dot(tri_ref[...], jnp.concatenate([la_hi, la_lo], axis=-1), preferred_element_type=F32)
    b_all = bsum[:, :GLA_K_WIDTH] + bsum[:, GLA_K_WIDTH:]

    row = lax.broadcasted_iota(jnp.int32, (GLA_HEADS * c, c), 0) % c
    col = lax.broadcasted_iota(jnp.int32, (GLA_HEADS * c, c), 1)
    causal = (row <= col) if reverse else (row >= col)
    hrow = lax.broadcasted_iota(jnp.int32, (GLA_WIDTH, GLA_K_WIDTH), 0) // GLA_DV
    hcol = lax.broadcasted_iota(jnp.int32, (GLA_WIDTH, GLA_K_WIDTH), 1) // GLA_DK
    blockdiag = hrow == hcol
    qlane_head = lax.broadcasted_iota(jnp.int32, (c, GLA_K_WIDTH), 1) // GLA_DK

    state = s_scr[...]
    order = range(nch - 1, -1, -1) if reverse else range(nch)
    for ci in order:
        r0 = ci * c
        bc = b_all[r0:r0 + c]
        if reverse:
            b_end, b_ref_row = bc[0:1], bc[c - 1 - c // 2:c - c // 2]
        else:
            b_end, b_ref_row = bc[c - 1:c], bc[c // 2:c // 2 + 1]
        qc = q_ref[0, r0:r0 + c, :].astype(F32) * (GLA_DK ** -0.5)
        kc = k_ref[0, r0:r0 + c, :].astype(F32)
        vc = v_ref[0, r0:r0 + c, :]
        qe = qc * jnp.exp(bc - b_ref_row)
        ke = (kc * jnp.exp(b_ref_row - bc)).astype(BF16)
        zq = jnp.zeros_like(qe)
        q_stack = jnp.concatenate([jnp.where(qlane_head == h, qe, zq) for h in range(GLA_HEADS)],
                                  axis=0).astype(BF16)
        att = lax.dot_general(q_stack, ke, (((1,), (1,)), ((), ())), preferred_element_type=F32)
        att = jnp.where(causal, att, 0.0).astype(BF16)
        o_intra = jnp.concatenate(
            [jnp.dot(att[h * c:(h + 1) * c], vc[:, h * GLA_DV:(h + 1) * GLA_DV], preferred_element_type=F32)
             for h in range(GLA_HEADS)], axis=-1)
        o_inter = lax.dot_general((qc * jnp.exp(bc)).astype(BF16), state.astype(BF16),
                                  (((1,), (1,)), ((), ())), preferred_element_type=F32)
        o_chunk = o_intra + o_inter
        if final:
            o_chunk = o_chunk + oprev_ref[0, r0:r0 + c, :]
            parts = []
            for h in range(GLA_HEADS):
                oh = o_chunk[:, h * GLA_DV:(h + 1) * GLA_DV]
                parts.append(oh * lax.rsqrt(jnp.mean(oh * oh, axis=-1, keepdims=True) + LN_EPS) * g_ref[...])
            rr = r_ref[0, r0:r0 + c, :].astype(F32)
            o_ref[0, r0:r0 + c, :] = (jnp.concatenate(parts, axis=-1)
                                      * (rr * jax.nn.sigmoid(rr))).astype(o_ref.dtype)
        else:
            o_ref[0, r0:r0 + c, :] = o_chunk.astype(o_ref.dtype)
        kd = (kc * jnp.exp(b_end - bc)).astype(BF16)
        v_t = jnp.transpose(vc.astype(F32)).astype(BF16)
        upd = jnp.dot(v_t, kd, preferred_element_type=F32)
        state = state * jnp.exp(b_end) + jnp.where(blockdiag, upd, 0.0)
    s_scr[...] = state
    sfin_ref[0] = state


def _gla_pass(pb, gz, wg_pad, bg, tri, s0, reverse, final_args=None):
    bsz, length, _ = pb.shape
    tb = min(length, 512)
    nb = length // tb
    blk = (lambda j: nb - 1 - j) if reverse else (lambda j: j)
    in_specs = [
        pl.BlockSpec((1, tb, GLA_K_WIDTH), lambda b, j: (b, blk(j), OFF_GQ // GLA_K_WIDTH)),
        pl.BlockSpec((1, tb, GLA_K_WIDTH), lambda b, j: (b, blk(j), OFF_GK // GLA_K_WIDTH)),
        pl.BlockSpec((1, tb, GLA_WIDTH), lambda b, j: (b, blk(j), OFF_GV // GLA_WIDTH)),
        pl.BlockSpec((1, tb, LANES), lambda b, j: (b, blk(j), 0)),
        pl.BlockSpec((LANES, GLA_K_WIDTH), lambda b, j: (0, 0)),
        pl.BlockSpec((1, GLA_K_WIDTH), lambda b, j: (0, 0)),
        pl.BlockSpec((tb, tb), lambda b, j: (0, 0)),
        pl.BlockSpec((1, GLA_WIDTH, GLA_K_WIDTH), lambda b, j: (b, 0, 0)),
    ]
    args = [pb, pb, pb, gz, wg_pad, bg, tri, s0]
    final = final_args is not None
    if final:
        o_prev, norm_g = final_args
        in_specs += [
            pl.BlockSpec((1, tb, GLA_WIDTH), lambda b, j: (b, blk(j), 0)),
            pl.BlockSpec((1, tb, GLA_WIDTH), lambda b, j: (b, blk(j), OFF_GR // GLA_WIDTH)),
            pl.BlockSpec((1, GLA_DV), lambda b, j: (0, 0)),
        ]
        args += [o_prev, pb, norm_g.reshape(1, GLA_DV)]
    return pl.pallas_call(
        functools.partial(_gla_kernel, reverse=reverse, final=final),
        grid=(bsz, nb),
        in_specs=in_specs,
        out_specs=[pl.BlockSpec((1, tb, GLA_WIDTH), lambda b, j: (b, blk(j), 0)),
                   pl.BlockSpec((1, GLA_WIDTH, GLA_K_WIDTH), lambda b, j: (b, 0, 0))],
        out_shape=[jax.ShapeDtypeStruct((bsz, length, GLA_WIDTH), BF16 if final else F32),
                   jax.ShapeDtypeStruct((bsz, GLA_WIDTH, GLA_K_WIDTH), F32)],
        scratch_shapes=[pltpu.VMEM((GLA_WIDTH, GLA_K_WIDTH), F32)],
        compiler_params=_cparams(("parallel", "arbitrary")),
        name="gla_bwd" if reverse else "gla_fwd",
    )(*args)


def _gla_consts(tb):
    tri = np.kron(np.eye(tb // GLA_CHUNK), np.tril(np.ones((GLA_CHUNK, GLA_CHUNK)))).astype(np.float32)
    return jnp.asarray(tri).astype(BF16), jnp.asarray(tri.T.copy()).astype(BF16)


def _slab_rows(n_slabs):
    return n_slabs * SLAB_PITCH


def _store_slabs(ref, lead, val, n_slabs):
    rows, lanes = val.shape
    for s in range(n_slabs):
        base = s * SLAB_PITCH
        lo = s * FFT_N1
        if lo + FFT_N1 <= rows:
            ref[lead + (slice(base, base + FFT_N1), slice(None))] = val[lo:lo + FFT_N1]
            ref[lead + (slice(base + FFT_N1, base + SLAB_PITCH), slice(None))] = jnp.zeros(
                (SLAB_PITCH - FFT_N1, lanes), val.dtype)
        else:
            ref[lead + (slice(base, base + SLAB_PITCH), slice(None))] = jnp.zeros((SLAB_PITCH, lanes), val.dtype)


def _hy_filter_kernel(feat_ref, w1_ref, b1_ref, f1_ref, w2_ref, b2_ref, f2_ref, w3_ref, dl_ref, o_ref, *, length):
    tl = feat_ref.shape[0]
    feat = feat_ref[...]
    h = jnp.sin(f1_ref[...] * (jnp.dot(feat, w1_ref[...], precision=HIGHEST, preferred_element_type=F32)
                               + b1_ref[...]))
    h = jnp.sin(f2_ref[...] * (jnp.dot(h, w2_ref[...], precision=HIGHEST, preferred_element_type=F32)
                               + b2_ref[...]))
    h = jnp.dot(h, w3_ref[...], precision=HIGHEST, preferred_element_type=F32)
    win = jnp.exp(-feat[:, 0:1] * dl_ref[...])
    pos = lax.broadcasted_iota(jnp.int32, win.shape, 0) + pl.program_id(0) * tl
    valid = pos < length
    taps_f = jnp.where(valid, h[:, :HY_WIDTH] * win, 0.0)
    taps_b = jnp.where(jnp.logical_and(valid, pos > 0), h[:, HY_WIDTH:] * win, 0.0)
    _store_slabs(o_ref, (0,), taps_f, tl // FFT_N1)
    _store_slabs(o_ref, (1,), taps_b, tl // FFT_N1)


def _hy_feats(length, lpad):
    t = jnp.linspace(0.0, 1.0, length, dtype=F32)[:, None]
    bands = (HY_POS_DIM - 1) // 2
    w = 2.0 * math.pi * jnp.arange(length, dtype=F32)[:, None] / length
    f = jnp.linspace(1e-4, bands - 1, bands, dtype=F32)[None, :]
    feat = jnp.concatenate([t, jnp.cos(f * w), -jnp.sin(f * w)], axis=-1)
    return jnp.zeros((lpad, LANES), F32).at[:length, :HY_POS_DIM].set(feat)


def _hy_filters(feat, length, p):
    lpad = feat.shape[0]
    tl = min(lpad, 512)
    max_decay = math.log(HY_TARGET) / HY_FAST_PCT
    min_decay = math.log(HY_TARGET) / HY_SLOW_PCT
    deltas = jnp.abs(jnp.linspace(min_decay, max_decay, HY_WIDTH, dtype=F32))[None, :]
    w1 = jnp.zeros((LANES, HY_HIDDEN), F32).at[:HY_POS_DIM].set(p['hy_w1'])
    full = lambda shape: pl.BlockSpec(shape, lambda i: (0,) * len(shape))
    tl_slab = _slab_rows(tl // FFT_N1)
    return pl.pallas_call(
        functools.partial(_hy_filter_kernel, length=length),
        grid=(lpad // tl,),
        in_specs=[pl.BlockSpec((tl, LANES), lambda i: (i, 0)),
                  full((LANES, HY_HIDDEN)), full((1, HY_HIDDEN)), full((1, HY_HIDDEN)),
                  full((HY_HIDDEN, HY_HIDDEN)), full((1, HY_HIDDEN)), full((1, HY_HIDDEN)),
                  full((HY_HIDDEN, 2 * HY_WIDTH)), full((1, HY_WIDTH))],
        out_specs=pl.BlockSpec((2, tl_slab, HY_WIDTH), lambda i: (0, i, 0)),
        out_shape=jax.ShapeDtypeStruct((2, _slab_rows(lpad // FFT_N1), HY_WIDTH), F32),
        compiler_params=_cparams(("parallel",)),
        name="hy_filter",
    )(feat, w1, p['hy_b1'].reshape(1, -1), p['hy_freq1'].reshape(1, -1), p['hy_w2'],
      p['hy_b2'].reshape(1, -1), p['hy_freq2'].reshape(1, -1), p['hy_w3'], deltas)


def _hy_pre_kernel(x0_ref, x1_ref, vv_ref, w0_ref, w1_ref, wv_ref, b0_ref, b1_ref, bv_ref, x0o_ref, u_ref, *, n_slabs):
    length = x0_ref.shape[1]
    row = lax.broadcasted_iota(jnp.int32, (length, x0_ref.shape[2]), 0)

    def conv(x_ref, w_ref, b_ref):
        x = x_ref[0].astype(F32)
        w = w_ref[...]
        xm = jnp.where(row == 0, 0.0, pltpu.roll(x, 1, axis=0))
        xp = jnp.where(row == length - 1, 0.0, pltpu.roll(x, length - 1, axis=0))
        return xm * w[0:1] + x * w[1:2] + xp * w[2:3] + b_ref[...]

    x0 = conv(x0_ref, w0_ref, b0_ref)
    u = conv(x1_ref, w1_ref, b1_ref) * conv(vv_ref, wv_ref, bv_ref)
    _store_slabs(x0o_ref, (0,), x0, n_slabs)
    _store_slabs(u_ref, (0,), u, n_slabs)


def _hy_pre(pb, conv_w, conv_b, n_slabs):
    bsz, length, _ = pb.shape
    nct = HY_WIDTH // LANES
    base = OFF_HY // LANES
    xs = lambda g: pl.BlockSpec((1, length, LANES), lambda b, j, g=g: (b, 0, base + g * nct + j))
    ws = lambda g: pl.BlockSpec((3, LANES), lambda b, j, g=g: (0, g * nct + j))
    bs = lambda g: pl.BlockSpec((1, LANES), lambda b, j, g=g: (0, g * nct + j))
    cb = conv_b.reshape(1, -1)
    rows = _slab_rows(n_slabs)
    return pl.pallas_call(
        functools.partial(_hy_pre_kernel, n_slabs=n_slabs),
        grid=(bsz, nct),
        in_specs=[xs(0), xs(1), xs(2), ws(0), ws(1), ws(2), bs(0), bs(1), bs(2)],
        out_specs=[pl.BlockSpec((1, rows, LANES), lambda b, j: (b, 0, j))] * 2,
        out_shape=[jax.ShapeDtypeStruct((bsz, rows, HY_WIDTH), F32)] * 2,
        compiler_params=_cparams(("parallel", "parallel")),
        name="hy_pre",
    )(pb, pb, pb, conv_w, conv_w, conv_w, cb, cb, cb)


def _fft_consts(n2, nin):
    n1 = FFT_N1
    n = n1 * n2
    k2 = np.arange(n2)[:, None]
    m2 = np.arange(nin)[None, :]
    ang_a = 2.0 * np.pi * (k2 * m2 % n2) / n2
    fa = np.concatenate([np.cos(ang_a), -np.sin(ang_a)], axis=0)
    ga = np.concatenate([np.cos(ang_a).T, -np.sin(ang_a).T], axis=1) / n
    k1 = np.arange(n1)[None, :, None]
    j1 = np.arange(n1)[None, None, :]
    kk2 = np.arange(n2)[:, None, None]
    ang_b = 2.0 * np.pi * ((j1 * k1 * n2 + j1 * kk2) % n) / n
    mr, mi = np.cos(ang_b), -np.sin(ang_b)
    big = np.concatenate([np.concatenate([mr, -mi], axis=2), np.concatenate([mi, mr], axis=2)], axis=1)
    big_t = np.transpose(big, (0, 2, 1))
    f32 = lambda a: jnp.asarray(a.astype(np.float32))
    bf16 = lambda a: jnp.asarray(a.astype(np.float32)).astype(BF16)
    return f32(fa), f32(ga), bf16(big), bf16(big_t)


def _fft_a_kernel(f_ref, x_ref, o_ref):
    nin = f_ref.shape[1]
    n_out = f_ref.shape[0]
    groups = x_ref.shape[0]
    f = f_ref[...]

    def body(n1, carry):
        for g in range(groups):
            xs = x_ref[g, pl.ds(n1, nin, stride=SLAB_PITCH), :]
            o_ref[g, pl.ds(n1, n_out, stride=SLAB_PITCH), :] = jnp.dot(f, xs, preferred_element_type=F32)
        return carry

    lax.fori_loop(0, FFT_N1, body, 0, unroll=8)
    zeros = jnp.zeros((n_out, x_ref.shape[2]), F32)
    for g in range(groups):
        for r in range(FFT_N1, SLAB_PITCH):
            o_ref[g, pl.ds(r, n_out, stride=SLAB_PITCH), :] = zeros


def _fft_stage_a(x, fa):
    g, rows, ch = x.shape
    n_out, nin = fa.shape
    assert rows == _slab_rows(nin)
    return pl.pallas_call(
        _fft_a_kernel,
        grid=(g, ch // LANES),
        in_specs=[pl.BlockSpec(fa.shape, lambda b, j: (0, 0)),
                  pl.BlockSpec((1, rows, LANES), lambda b, j: (b, 0, j))],
        out_specs=pl.BlockSpec((1, _slab_rows(n_out), LANES), lambda b, j: (b, 0, j)),
        out_shape=jax.ShapeDtypeStruct((g, _slab_rows(n_out), ch), F32),
        compiler_params=_cparams(("parallel", "parallel")),
        name="fft_stage_a",
    )(fa, x)


def _fft_spec_kernel(m_ref, a_ref, o_ref):
    big = m_ref[0]
    xf = jnp.dot(big, jnp.concatenate([a_ref[0, 0, 0], a_ref[0, 1, 0]], axis=0).astype(BF16),
                 preferred_element_type=F32)
    xb = jnp.dot(big, jnp.concatenate([a_ref[1, 0, 0], a_ref[1, 1, 0]], axis=0).astype(BF16),
                 preferred_element_type=F32)
    o_ref[0, 0] = xf[:FFT_N1] + xb[:FFT_N1]
    o_ref[1, 0] = xf[FFT_N1:] - xb[FFT_N1:]


def _filter_spectrum(a_taps, big, n2):
    ch = a_taps.shape[-1]
    av = a_taps.reshape(2, 2, n2, SLAB_PITCH, ch)
    return pl.pallas_call(
        _fft_spec_kernel,
        grid=(n2,),
        in_specs=[pl.BlockSpec((1, 2 * FFT_N1, 2 * FFT_N1), lambda k: (k, 0, 0)),
                  pl.BlockSpec((2, 2, 1, FFT_N1, ch), lambda k: (0, 0, k, 0, 0))],
        out_specs=pl.BlockSpec((2, 1, FFT_N1, ch), lambda k: (0, k, 0, 0)),
        out_shape=jax.ShapeDtypeStruct((2, n2, FFT_N1, ch), F32),
        compiler_params=_cparams(("parallel",)),
        name="fft_filter_spec",
    )(big, av)


def _fft_conv_kernel(m_ref, mt_ref, h_ref, a_ref, o_ref):
    hr, hi = h_ref[0, 0], h_ref[1, 0]
    for b in range(a_ref.shape[0]):
        x = jnp.dot(m_ref[0], jnp.concatenate([a_ref[b, 0, 0], a_ref[b, 1, 0]], axis=0).astype(BF16),
                    preferred_element_type=F32)
        xr, xi = x[:FFT_N1], x[FFT_N1:]
        y = jnp.concatenate([xr * hr - xi * hi, xr * hi + xi * hr], axis=0).astype(BF16)
        t = jnp.dot(mt_ref[0], y, preferred_element_type=F32)
        o_ref[b, 0, 0] = t[:FFT_N1]
        o_ref[b, 1, 0] = t[FFT_N1:]


def _fft_conv_mid(a_sig, spec, big, big_t, n2):
    bsz = a_sig.shape[0]
    ch = a_sig.shape[-1]
    av = a_sig.reshape(bsz, 2, n2, SLAB_PITCH, ch)
    out = pl.pallas_call(
        _fft_conv_kernel,
        grid=(n2,),
        in_specs=[pl.BlockSpec((1, 2 * FFT_N1, 2 * FFT_N1), lambda k: (k, 0, 0)),
                  pl.BlockSpec((1, 2 * FFT_N1, 2 * FFT_N1), lambda k: (k, 0, 0)),
                  pl.BlockSpec((2, 1, FFT_N1, ch), lambda k: (0, k, 0, 0)),
                  pl.BlockSpec((bsz, 2, 1, FFT_N1, ch), lambda k: (0, 0, k, 0, 0))],
        out_specs=pl.BlockSpec((bsz, 2, 1, FFT_N1, ch), lambda k: (0, 0, k, 0, 0)),
        out_shape=jax.ShapeDtypeStruct((bsz, 2, n2, SLAB_PITCH, ch), F32),
        compiler_params=_cparams(("parallel",)),
        name="fft_conv_mid",
    )(big, big_t, spec, av)
    return out.reshape(bsz, 2 * n2 * SLAB_PITCH, ch)


def _fft_out_kernel(g_ref, t_ref, x0_ref, u_ref, skip_ref, o_ref):
    nin, n_in = g_ref.shape
    g = g_ref[...]
    skip = skip_ref[...]

    def body(n1, carry):
        ts = t_ref[0, pl.ds(n1, n_in, stride=SLAB_PITCH), :]
        y = jnp.dot(g, ts, preferred_element_type=F32)
        x0 = x0_ref[0, pl.ds(n1, nin, stride=SLAB_PITCH), :]
        u = u_ref[0, pl.ds(n1, nin, stride=SLAB_PITCH), :]
        o_ref[0, pl.ds(n1, nin, stride=FFT_N1), :] = x0 * (y + skip * u)
        return carry

    lax.fori_loop(0, FFT_N1, body, 0, unroll=8)


def _fft_out(t, ga, x0, u, skip):
    bsz, rows, ch = u.shape
    nin, n_in = ga.shape
    return pl.pallas_call(
        _fft_out_kernel,
        grid=(bsz, ch // LANES),
        in_specs=[pl.BlockSpec(ga.shape, lambda b, j: (0, 0)),
                  pl.BlockSpec((1, _slab_rows(n_in), LANES), lambda b, j: (b, 0, j)),
                  pl.BlockSpec((1, rows, LANES), lambda b, j: (b, 0, j)),
                  pl.BlockSpec((1, rows, LANES), lambda b, j: (b, 0, j)),
                  pl.BlockSpec((1, LANES), lambda b, j: (0, j))],
        out_specs=pl.BlockSpec((1, nin * FFT_N1, LANES), lambda b, j: (b, 0, j)),
        out_shape=jax.ShapeDtypeStruct((bsz, nin * FFT_N1, ch), F32),
        compiler_params=_cparams(("parallel", "parallel")),
        name="fft_out",
    )(ga, t, x0, u, skip.reshape(1, ch))


def _hyena(pb, p, consts, length):
    fa, ga, big, big_t, feat, n2, nin = consts
    taps = _hy_filters(feat, length, p)
    spec = _filter_spectrum(_fft_stage_a(taps, fa), big, n2)
    x0, u = _hy_pre(pb, p['hy_conv_w'], p['hy_conv_b'], nin)
    t = _fft_conv_mid(_fft_stage_a(u, fa), spec, big, big_t, n2)
    y = _fft_out(t, ga, x0, u, p['hy_skip'])
    return y if y.shape[1] == length else y[:, :length]


def _hyena_consts(length):
    if length >= 2048:
        lpad, n2 = length, 2 * length // FFT_N1
    else:
        lpad = max(4 * length, 1024)
        n2 = lpad // FFT_N1
    nin = lpad // FFT_N1
    return _fft_consts(n2, nin) + (_hy_feats(length, lpad), n2, nin)


def _merge_kernel(ya_ref, yb_ref, yc_ref, lg_ref, x_ref, gate_ref, lng_ref, lnb_ref, wb_ref, wo_ref, o_ref):
    lg = lg_ref[0].astype(F32)
    ys = (ya_ref[0], yb_ref[0], yc_ref[0].astype(BF16))
    mix = None
    for g in range(N_BRANCH):
        proj = jnp.dot(ys[g], wb_ref[g], preferred_element_type=F32)
        term = jax.nn.sigmoid(lg[:, g * D_MODEL:(g + 1) * D_MODEL]) * proj
        mix = term if mix is None else mix + term
    y = jnp.dot(mix.astype(BF16), wo_ref[...], preferred_element_type=F32)
    z = DEEPNORM_ALPHA * x_ref[0] + gate_ref[0] * y
    o_ref[0] = _layer_norm_f32(z) * lng_ref[...] + lnb_ref[...]


def _merge(ya, yb, yc, pb, x, gate, ln_g, ln_b, w_branch, w_out):
    bsz, length, d = x.shape
    tm = min(length, 512)
    return pl.pallas_call(
        _merge_kernel,
        grid=(bsz, length // tm),
        in_specs=[pl.BlockSpec((1, tm, 512), lambda b, i: (b, i, 0)),
                  pl.BlockSpec((1, tm, 512), lambda b, i: (b, i, 0)),
                  pl.BlockSpec((1, tm, 512), lambda b, i: (b, i, 0)),
                  pl.BlockSpec((1, tm, N_BRANCH * d), lambda b, i: (b, i, OFF_GATE // (N_BRANCH * d))),
                  pl.BlockSpec((1, tm, d), lambda b, i: (b, i, 0)),
                  pl.BlockSpec((1, 1, d), lambda b, i: (b, 0, 0)),
                  pl.BlockSpec((1, d), lambda b, i: (0, 0)),
                  pl.BlockSpec((1, d), lambda b, i: (0, 0)),
                  pl.BlockSpec((N_BRANCH, 512, d), lambda b, i: (0, 0, 0)),
                  pl.BlockSpec((d, d), lambda b, i: (0, 0))],
        out_specs=pl.BlockSpec((1, tm, d), lambda b, i: (b, i, 0)),
        out_shape=jax.ShapeDtypeStruct((bsz, length, d), F32),
        compiler_params=_cparams(("parallel", "parallel")),
        name="merge",
    )(ya, yb, yc, pb, x, gate, ln_g.reshape(1, d), ln_b.reshape(1, d), w_branch, w_out)


FF_CHUNK = 256


def _gelu_tanh(x):
    return 0.5 * x * (1.0 + jnp.tanh(math.sqrt(2.0 / math.pi) * (x + 0.044715 * (x * x * x))))


def _ffn_kernel(xp_ref, x_ref, xn_ref, sh_ref, sc_ref, gate_ref, lng_ref, lnb_ref,
                wup_ref, cw_ref, cb_ref, wdn_ref, o_ref, h_scr, acc_scr):
    i = pl.program_id(1)
    nt = pl.num_programs(1)
    tm = x_ref.shape[1]
    sc, sh = 1.0 + sc_ref[0], sh_ref[0]
    x = x_ref[0]
    h_scr[0:8, :] = _layer_norm_f32(xp_ref[0]) * sc + sh
    h_scr[8:8 + tm, :] = _layer_norm_f32(x) * sc + sh
    h_scr[8 + tm:16 + tm, :] = _layer_norm_f32(xn_ref[0]) * sc + sh
    hext = h_scr[...].astype(BF16)

    row = lax.broadcasted_iota(jnp.int32, (tm, FF_CHUNK), 0)
    keep_prev = jnp.logical_or(row > 0, i > 0)
    keep_next = jnp.logical_or(row < tm - 1, i < nt - 1)

    def conv(up, w, b):
        um = jnp.where(keep_prev, up[7:7 + tm], 0.0)
        un = jnp.where(keep_next, up[9:9 + tm], 0.0)
        return um * w[0:1] + up[8:8 + tm] * w[1:2] + un * w[2:3] + b

    n_chunks = D_FF // FF_CHUNK
    for cidx in range(n_chunks):
        ca = cidx * FF_CHUNK
        cg = D_FF + cidx * FF_CHUNK
        up_a = jnp.dot(hext, wup_ref[:, ca:ca + FF_CHUNK], preferred_element_type=F32)
        up_g = jnp.dot(hext, wup_ref[:, cg:cg + FF_CHUNK], preferred_element_type=F32)
        a = conv(up_a, cw_ref[:, ca:ca + FF_CHUNK], cb_ref[:, ca:ca + FF_CHUNK])
        g = conv(up_g, cw_ref[:, cg:cg + FF_CHUNK], cb_ref[:, cg:cg + FF_CHUNK])
        act = (_gelu_tanh(g) * a).astype(BF16)
        part = jnp.dot(act, wdn_ref[ca:ca + FF_CHUNK, :], preferred_element_type=F32)
        if cidx == 0:
            acc_scr[...] = part
        else:
            acc_scr[...] += part
    z = DEEPNORM_ALPHA * x + gate_ref[0] * acc_scr[...]
    o_ref[0] = _layer_norm_f32(z) * lng_ref[...] + lnb_ref[...]


def _ffn(x, shift, scale, gate, ln_g, ln_b, w_up, conv_w, conv_b, w_down):
    bsz, length, d = x.shape
    tm = min(length, 512)
    nt = length // tm
    r8 = tm // 8
    last8 = length // 8 - 1
    vec = pl.BlockSpec((1, 1, d), lambda b, i: (b, 0, 0))
    row = pl.BlockSpec((1, d), lambda b, i: (0, 0))
    return pl.pallas_call(
        _ffn_kernel,
        grid=(bsz, nt),
        in_specs=[pl.BlockSpec((1, 8, d), lambda b, i: (b, jnp.maximum(i * r8 - 1, 0), 0)),
                  pl.BlockSpec((1, tm, d), lambda b, i: (b, i, 0)),
                  pl.BlockSpec((1, 8, d), lambda b, i: (b, jnp.minimum((i + 1) * r8, last8), 0)),
                  vec, vec, vec, row, row,
                  pl.BlockSpec((d, 2 * D_FF), lambda b, i: (0, 0)),
                  pl.BlockSpec((3, 2 * D_FF), lambda b, i: (0, 0)),
                  pl.BlockSpec((1, 2 * D_FF), lambda b, i: (0, 0)),
                  pl.BlockSpec((D_FF, d), lambda b, i: (0, 0))],
        out_specs=pl.BlockSpec((1, tm, d), lambda b, i: (b, i, 0)),
        out_shape=jax.ShapeDtypeStruct((bsz, length, d), F32),
        scratch_shapes=[pltpu.VMEM((tm + 16, d), F32), pltpu.VMEM((tm, d), F32)],
        compiler_params=_cparams(("parallel", "parallel")),
        name="conv_ffn",
    )(x, x, x, shift, scale, gate, ln_g.reshape(1, d), ln_b.reshape(1, d),
      w_up, conv_w, conv_b.reshape(1, -1), w_down)


def _prep_w_in(w_in):
    a_q, a_k, a_v, g_q, g_k, g_v, g_r, g_z, hy, gate = jnp.split(
        w_in, [512, 1024, 1536, 1792, 2048, 2560, 3072, 3104, 4640], axis=-1)
    wb = jnp.concatenate([a_q, a_k, a_v, g_q, g_k, g_v, g_r, gate, hy], axis=-1).astype(BF16)
    wz = jnp.concatenate([g_z, jnp.zeros((w_in.shape[0], LANES - 2 * GLA_GATE_RANK), w_in.dtype)],
                         axis=-1).astype(BF16)
    return wb, wz


def _prep_gate(w_gate, b_gate):
    pads = []
    for d in range(2):
        m = jnp.zeros((LANES, GLA_K_WIDTH), F32).at[d * GLA_GATE_RANK:(d + 1) * GLA_GATE_RANK].set(w_gate[d])
        pads.append(m)
    return pads, [b_gate[0].reshape(1, -1), b_gate[1].reshape(1, -1)]


def kernel(x, c, ctx, c_ctx, w_ada, b_ada, w_in, da_lambda, da_norm_g, gla_w_gate, gla_b_gate, gla_norm_g,
           hy_conv_w, hy_conv_b, hy_w1, hy_b1, hy_freq1, hy_w2, hy_b2, hy_freq2, hy_w3, hy_skip, w_branch,
           w_out, ln1_g, ln1_b, ffn_w_up, ffn_conv_w, ffn_conv_b, ffn_w_down, ln2_g, ln2_b):
    bsz, length, d = x.shape
    c_len = ctx.shape[1]
    depth = w_ada.shape[0]

    cc = jnp.zeros((8, d), F32).at[:bsz].set(c).at[bsz].set(c_ctx)
    mod = _ada_all(cc, w_ada, b_ada)

    rope = _rope_tables(length)
    hy_lat = _hyena_consts(length)
    hy_ctx = _hyena_consts(c_len)
    tri_lat = _gla_consts(min(length, 512))
    tri_ctx = _gla_consts(min(c_len, 512))
    zero_state = jnp.zeros((bsz, GLA_WIDTH, GLA_K_WIDTH), F32)

    x_lat, x_ctx = x, ctx
    for l in range(depth):
        with_ctx = l < depth - 1
        lam_init = 0.8 - 0.6 * math.exp(-0.3 * l)
        p = {'hy_conv_w': hy_conv_w[l], 'hy_conv_b': hy_conv_b[l], 'hy_w1': hy_w1[l], 'hy_b1': hy_b1[l],
             'hy_freq1': hy_freq1[l], 'hy_w2': hy_w2[l], 'hy_b2': hy_b2[l], 'hy_freq2': hy_freq2[l],
             'hy_w3': hy_w3[l], 'hy_skip': hy_skip[l]}
        wb, wz = _prep_w_in(w_in[l])
        wg_pad, bg = _prep_gate(gla_w_gate[l], gla_b_gate[l])
        wbr = w_branch[l].astype(BF16)
        wo = w_out[l].astype(BF16)
        wup = ffn_w_up[l].astype(BF16)
        wdn = ffn_w_down[l].astype(BF16)

        m_lat = mod[l, :bsz].reshape(bsz, 1, 6 * d)
        m_ctx = jnp.broadcast_to(mod[l, bsz].reshape(1, 1, 6 * d), (bsz, 1, 6 * d))
        sh1, sc1, g1, sh2, sc2, g2 = [m_lat[..., k * d:(k + 1) * d] for k in range(6)]
        csh1, csc1, cg1, csh2, csc2, cg2 = [m_ctx[..., k * d:(k + 1) * d] for k in range(6)]

        pb, gz = _inproj(x_lat, sh1, sc1, wb, wz)
        cpb, cgz = _inproj(x_ctx, csh1, csc1, wb, wz)

        y_a = _diff_attention(pb, [cpb, pb], da_lambda[l], da_norm_g[l], lam_init, rope=rope)
        co_f, cs_f = _gla_pass(cpb, cgz, wg_pad[0], bg[0], tri_ctx[0], zero_state, False)
        if with_ctx:
            y_cb, cs_b = _gla_pass(cpb, cgz, wg_pad[1], bg[1], tri_ctx[1], zero_state, True,
                                   final_args=(co_f, gla_norm_g[l]))
        else:
            _, cs_b = _gla_pass(cpb, cgz, wg_pad[1], bg[1], tri_ctx[1], zero_state, True)
        o_f, _ = _gla_pass(pb, gz, wg_pad[0], bg[0], tri_lat[0], cs_f, False)
        y_b, _ = _gla_pass(pb, gz, wg_pad[1], bg[1], tri_lat[1], cs_b, True, final_args=(o_f, gla_norm_g[l]))
        y_c = _hyena(pb, p, hy_lat, length)

        x_lat = _merge(y_a, y_b, y_c, pb, x_lat, g1, ln1_g[l], ln1_b[l], wbr, wo)
        x_lat = _ffn(x_lat, sh2, sc2, g2, ln2_g[l], ln2_b[l], wup, ffn_conv_w[l], ffn_conv_b[l], wdn)
        if with_ctx:
            y_ca = _diff_attention(cpb, [cpb], da_lambda[l], da_norm_g[l], lam_init)
            y_cc = _hyena(cpb, p, hy_ctx, c_len)
            x_ctx = _merge(y_ca, y_cb, y_cc, cpb, x_ctx, cg1, ln1_g[l], ln1_b[l], wbr, wo)
            x_ctx = _ffn(x_ctx, csh2, csc2, cg2, ln2_g[l], ln2_b[l], wup, ffn_conv_w[l], ffn_conv_b[l], wdn)
    return x_lat
```

```python
import functools
import math

import numpy as np
import jax
import jax.numpy as jnp
from jax import lax
from jax.experimental import pallas as pl
from jax.experimental.pallas import tpu as pltpu

F32 = jnp.float32
BF16 = jnp.bfloat16
HIGHEST = lax.Precision.HIGHEST

D_MODEL = 1024
DEPTH = 4
GRID_W = 64
DA_HEADS = 4
DA_HEAD_DIM = 64
DA_V_DIM = 128
ROPE_THETA = 10000.0
GLA_HEADS = 4
GLA_DK = 64
GLA_DV = 128
GLA_K_WIDTH = 256
GLA_WIDTH = 512
GLA_GATE_RANK = 16
GLA_GATE_TAU = 16.0
GLA_CHUNK = 64
HY_WIDTH = 512
HY_POS_DIM = 33
HY_HIDDEN = 64
HY_TARGET = 1e-2
HY_FAST_PCT = 0.3
HY_SLOW_PCT = 1.5
N_BRANCH = 3
D_FF = 2816
LN_EPS = 1e-5
DEEPNORM_ALPHA = (2.0 * DEPTH) ** 0.25

NB_COLS = 7680
OFF_AQ, OFF_AK, OFF_AV, OFF_GQ, OFF_GK, OFF_GV, OFF_GR, OFF_GATE, OFF_HY = (
    0, 512, 1024, 1536, 1792, 2048, 2560, 3072, 6144)

LANES = 128
FFT_N1 = 128
SLAB_PITCH = 136
VMEM_LIMIT = 56 * 1024 * 1024


def _cparams(sem):
    return pltpu.CompilerParams(dimension_semantics=sem, vmem_limit_bytes=VMEM_LIMIT)


def _layer_norm_f32(x):
    mu = jnp.mean(x, axis=-1, keepdims=True)
    xc = x - mu
    var = jnp.mean(xc * xc, axis=-1, keepdims=True)
    return xc * lax.rsqrt(var + LN_EPS)


def _ada_kernel(c_ref, w_ref, b_ref, o_ref):
    cc = c_ref[...]
    s = cc * jax.nn.sigmoid(cc)
    o_ref[0] = jnp.dot(s, w_ref[0], precision=HIGHEST, preferred_element_type=F32) + b_ref[0]


def _ada_all(cc, w_ada, b_ada):
    depth, d, n = w_ada.shape
    tn = 1536
    return pl.pallas_call(
        _ada_kernel,
        grid=(depth, n // tn),
        in_specs=[
            pl.BlockSpec((8, d), lambda l, j: (0, 0)),
            pl.BlockSpec((1, d, tn), lambda l, j: (l, 0, j)),
            pl.BlockSpec((1, 1, tn), lambda l, j: (l, 0, j)),
        ],
        out_specs=pl.BlockSpec((1, 8, tn), lambda l, j: (l, 0, j)),
        out_shape=jax.ShapeDtypeStruct((depth, 8, n), F32),
        compiler_params=_cparams(("parallel", "parallel")),
        name="ada",
    )(cc, w_ada, b_ada.reshape(depth, 1, n))


def _inproj_kernel(x_ref, sh_ref, sc_ref, w_ref, wz_ref, o_ref, z_ref, h_scr):
    @pl.when(pl.program_id(2) == 0)
    def _():
        h = (_layer_norm_f32(x_ref[0]) * (1.0 + sc_ref[0]) + sh_ref[0]).astype(BF16)
        h_scr[...] = h
        z_ref[0] = jnp.dot(h, wz_ref[...], preferred_element_type=F32)

    o_ref[0] = jnp.dot(h_scr[...], w_ref[...], preferred_element_type=F32).astype(o_ref.dtype)


def _inproj(x, shift, scale, w, wz):
    bsz, length, d = x.shape
    n_cols = w.shape[1]
    tm = min(length, 1024)
    tn = 1536
    return pl.pallas_call(
        _inproj_kernel,
        grid=(bsz, length // tm, n_cols // tn),
        in_specs=[
            pl.BlockSpec((1, tm, d), lambda b, i, j: (b, i, 0)),
            pl.BlockSpec((1, 1, d), lambda b, i, j: (b, 0, 0)),
            pl.BlockSpec((1, 1, d), lambda b, i, j: (b, 0, 0)),
            pl.BlockSpec((d, tn), lambda b, i, j: (0, j)),
            pl.BlockSpec((d, LANES), lambda b, i, j: (0, 0)),
        ],
        out_specs=[pl.BlockSpec((1, tm, tn), lambda b, i, j: (b, i, j)),
                   pl.BlockSpec((1, tm, LANES), lambda b, i, j: (b, i, 0))],
        out_shape=[jax.ShapeDtypeStruct((bsz, length, n_cols), BF16),
                   jax.ShapeDtypeStruct((bsz, length, LANES), F32)],
        scratch_shapes=[pltpu.VMEM((tm, d), BF16)],
        compiler_params=_cparams(("parallel", "parallel", "arbitrary")),
        name="inproj",
    )(x, shift, scale, w, wz)


def _rope_tables(length):
    n_freq = DA_HEAD_DIM // 4
    inv = ROPE_THETA ** (-jnp.arange(n_freq, dtype=F32) / n_freq)
    rows = jnp.repeat(jnp.arange(length // GRID_W), GRID_W).astype(F32)
    cols = (jnp.arange(length) % GRID_W).astype(F32)
    ang_r = rows[:, None] * inv
    ang_c = cols[:, None] * inv
    cos64 = jnp.concatenate([jnp.cos(ang_r), jnp.cos(ang_r), jnp.cos(ang_c), jnp.cos(ang_c)], axis=-1)
    sin64 = jnp.concatenate([-jnp.sin(ang_r), jnp.sin(ang_r), -jnp.sin(ang_c), jnp.sin(ang_c)], axis=-1)
    return jnp.concatenate([cos64, cos64], axis=-1), jnp.concatenate([sin64, sin64], axis=-1)


def _swap16(x):
    n = x.shape[-1]
    lane = lax.broadcasted_iota(jnp.int32, x.shape, x.ndim - 1)
    up = pltpu.roll(x, n - 16, axis=x.ndim - 1)
    dn = pltpu.roll(x, 16, axis=x.ndim - 1)
    return jnp.where((lane & 16) == 0, up, dn)


def _rope(x, cos, sin):
    return x * cos + _swap16(x) * sin


ATTN_TK = 512
ROPE_ROWS = 512


def _attn_kernel(*refs, n_src, rope, lam_init):
    lam_ref, g_ref, q_ref = refs[:3]
    pos = 3
    if rope:
        qcos_ref, qsin_ref, kcos_ref, ksin_ref = refs[pos:pos + 4]
        pos += 4
    kv_refs = refs[pos:pos + 2 * n_src]
    o_ref = refs[pos + 2 * n_src]
    scr = refs[pos + 2 * n_src + 1:]
    vx_scr = scr[:n_src]
    krot_scr = scr[n_src] if rope else None
    tq = q_ref.shape[1]

    @pl.when(pl.program_id(2) == 0)
    def _():
        for s_idx in range(n_src):
            v_ref = kv_refs[2 * s_idx + 1]
            n_keys = v_ref.shape[1]
            vx_scr[s_idx][:, 0:DA_V_DIM] = v_ref[0]
            vx_scr[s_idx][:, DA_V_DIM:2 * DA_V_DIM] = jnp.ones((n_keys, DA_V_DIM), BF16)
        if rope:
            k_ref = kv_refs[2 * (n_src - 1)]
            rr = min(ROPE_ROWS, k_ref.shape[1])

            def rot(j, carry):
                r0 = pl.multiple_of(j * rr, rr)
                kk = k_ref[0, pl.ds(r0, rr), :].astype(F32)
                krot_scr[pl.ds(r0, rr), :] = _rope(kk, kcos_ref[pl.ds(r0, rr), :],
                                                   ksin_ref[pl.ds(r0, rr), :]).astype(BF16)
                return carry

            lax.fori_loop(0, k_ref.shape[1] // rr, rot, 0)

    lp = lam_ref[...]
    lam = (jnp.exp(jnp.sum(lp[0] * lp[1], axis=-1, keepdims=True))
           - jnp.exp(jnp.sum(lp[2] * lp[3], axis=-1, keepdims=True)) + lam_init)

    q = q_ref[0].astype(F32)
    if rope:
        q = _rope(q, qcos_ref[...], qsin_ref[...])
    q = (q * (DA_HEAD_DIM ** -0.5 * math.log2(math.e))).astype(BF16)
    lane = lax.broadcasted_iota(jnp.int32, q.shape, 1)
    zero = jnp.zeros_like(q)
    qs = jnp.concatenate([jnp.where(lane < DA_HEAD_DIM, q, zero),
                          jnp.where(lane >= DA_HEAD_DIM, q, zero)], axis=0)

    m = jnp.full((2 * tq, 1), -1e30, F32)
    acc = jnp.zeros((2 * tq, 2 * DA_V_DIM), F32)
    for s_idx in range(n_src):
        k_ref = kv_refs[2 * s_idx]
        k_src = krot_scr if (rope and s_idx == n_src - 1) else k_ref.at[0]
        n_keys = k_ref.shape[1]
        tk = min(ATTN_TK, n_keys)
        for j in range(n_keys // tk):
            kb = k_src[j * tk:(j + 1) * tk, :]
            vb = vx_scr[s_idx][j * tk:(j + 1) * tk, :]
            s = lax.dot_general(qs, kb, (((1,), (1,)), ((), ())), preferred_element_type=F32)
            m_new = jnp.maximum(m, jnp.max(s, axis=-1, keepdims=True))
            p = jnp.exp2(s - m_new).astype(BF16)
            acc = jnp.exp2(m - m_new) * acc + jnp.dot(p, vb, preferred_element_type=F32)
            m = m_new

    o = acc[:, :DA_V_DIM] / acc[:, DA_V_DIM:]
    a = o[:tq] - lam * o[tq:]
    y = a * lax.rsqrt(jnp.mean(a * a, axis=-1, keepdims=True) + LN_EPS) * g_ref[...]
    o_ref[0] = (y * (1.0 - lam_init)).astype(o_ref.dtype)


def _diff_attention(q_arr, kv_arrs, lam_p, norm_g, lam_init, rope=None):
    bsz, lq, _ = q_arr.shape
    tq = min(lq, 512)
    in_specs = [
        pl.BlockSpec((4, 1, DA_HEAD_DIM), lambda b, h, i: (0, 0, 0)),
        pl.BlockSpec((1, DA_V_DIM), lambda b, h, i: (0, 0)),
        pl.BlockSpec((1, tq, LANES), lambda b, h, i: (b, i, OFF_AQ // LANES + h)),
    ]
    args = [lam_p.reshape(4, 1, DA_HEAD_DIM), norm_g.reshape(1, DA_V_DIM), q_arr]
    if rope is not None:
        cos, sin = rope
        in_specs += [pl.BlockSpec((tq, LANES), lambda b, h, i: (i, 0))] * 2
        in_specs += [pl.BlockSpec((lq, LANES), lambda b, h, i: (0, 0))] * 2
        args += [cos, sin, cos, sin]
    scratch = []
    for arr in kv_arrs:
        lk = arr.shape[1]
        in_specs += [pl.BlockSpec((1, lk, LANES), lambda b, h, i: (b, 0, OFF_AK // LANES + h)),
                     pl.BlockSpec((1, lk, LANES), lambda b, h, i: (b, 0, OFF_AV // LANES + h))]
        args += [arr, arr]
        scratch.append(pltpu.VMEM((lk, 2 * DA_V_DIM), BF16))
    if rope is not None:
        scratch.append(pltpu.VMEM((kv_arrs[-1].shape[1], LANES), BF16))
    return pl.pallas_call(
        functools.partial(_attn_kernel, n_src=len(kv_arrs), rope=rope is not None, lam_init=lam_init),
        grid=(bsz, DA_HEADS, lq // tq),
        in_specs=in_specs,
        out_specs=pl.BlockSpec((1, tq, LANES), lambda b, h, i: (b, i, h)),
        out_shape=jax.ShapeDtypeStruct((bsz, lq, DA_HEADS * DA_V_DIM), BF16),
        scratch_shapes=scratch,
        compiler_params=_cparams(("parallel", "parallel", "arbitrary")),
        name="diff_attn",
    )(*args)


def _gla_kernel(*refs, reverse, final):
    if final:
        (q_ref, k_ref, v_ref, z_ref, wg_ref, bg_ref, tri_ref, s0_ref, oprev_ref, r_ref, g_ref,
         o_ref, sfin_ref, s_scr) = refs
    else:
        (q_ref, k_ref, v_ref, z_ref, wg_ref, bg_ref, tri_ref, s0_ref,
         o_ref, sfin_ref, s_scr) = refs
    j = pl.program_id(1)
    tb = q_ref.shape[1]
    nch = tb // GLA_CHUNK
    c = GLA_CHUNK

    @pl.when(j == 0)
    def _():
        s_scr[...] = s0_ref[0]

    pre = jnp.dot(z_ref[0], wg_ref[...], precision=HIGHEST, preferred_element_type=F32) + bg_ref[...]
    la = (jnp.minimum(pre, 0.0) - jnp.log1p(jnp.exp(-jnp.abs(pre)))) * (1.0 / GLA_GATE_TAU)
    la_hi = la.astype(BF16)
    la_lo = (la - la_hi.astype(F32)).astype(BF16)
    bsum = jnp.dot(tri_ref[...], jnp.concatenate([la_hi, la_lo], axis=-1), preferred_element_type=F32)
    b_all = bsum[:, :GLA_K_WIDTH] + bsum[:, GLA_K_WIDTH:]

    row = lax.broadcasted_iota(jnp.int32, (GLA_HEADS * c, c), 0) % c
    col = lax.broadcasted_iota(jnp.int32, (GLA_HEADS * c, c), 1)
    causal = (row <= col) if reverse else (row >= col)
    hrow = lax.broadcasted_iota(jnp.int32, (GLA_WIDTH, GLA_K_WIDTH), 0) // GLA_DV
    hcol = lax.broadcasted_iota(jnp.int32, (GLA_WIDTH, GLA_K_WIDTH), 1) // GLA_DK
    blockdiag = hrow == hcol
    qlane_head = lax.broadcasted_iota(jnp.int32, (c, GLA_K_WIDTH), 1) // GLA_DK

    state = s_scr[...]
    order = range(nch - 1, -1, -1) if reverse else range(nch)
    for ci in order:
        r0 = ci * c
        bc = b_all[r0:r0 + c]
        if reverse:
            b_end, b_ref_row = bc[0:1], bc[c - 1 - c // 2:c - c // 2]
        else:
            b_end, b_ref_row = bc[c - 1:c], bc[c // 2:c // 2 + 1]
        qc = q_ref[0, r0:r0 + c, :].astype(F32) * (GLA_DK ** -0.5)
        kc = k_ref[0, r0:r0 + c, :].astype(F32)
        vc = v_ref[0, r0:r0 + c, :]
        qe = qc * jnp.exp(bc - b_ref_row)
        ke = (kc * jnp.exp(b_ref_row - bc)).astype(BF16)
        zq = jnp.zeros_like(qe)
        q_stack = jnp.concatenate([jnp.where(qlane_head == h, qe, zq) for h in range(GLA_HEADS)],
                                  axis=0).astype(BF16)
        att = lax.dot_general(q_stack, ke, (((1,), (1,)), ((), ())), preferred_element_type=F32)
        att = jnp.where(causal, att, 0.0).astype(BF16)
        o_intra = jnp.concatenate(
            [jnp.dot(att[h * c:(h + 1) * c], vc[:, h * GLA_DV:(h + 1) * GLA_DV], preferred_element_type=F32)
             for h in range(GLA_HEADS)], axis=-1)
        o_inter = lax.dot_general((qc * jnp.exp(bc)).astype(BF16), state.astype(BF16),
                                  (((1,), (1,)), ((), ())), preferred_element_type=F32)
        o_chunk = o_intra + o_inter
        if final:
            o_chunk = o_chunk + oprev_ref[0, r0:r0 + c, :]
            parts = []
            for h in range(GLA_HEADS):
                oh = o_chunk[:, h * GLA_DV:(h + 1) * GLA_DV]
                parts.append(oh * lax.rsqrt(jnp.mean(oh * oh, axis=-1, keepdims=True) + LN_EPS) * g_ref[...])
            rr = r_ref[0, r0:r0 + c, :].astype(F32)
            o_ref[0, r0:r0 + c, :] = (jnp.concatenate(parts, axis=-1)
                                      * (rr * jax.nn.sigmoid(rr))).astype(o_ref.dtype)
        else:
            o_ref[0, r0:r0 + c, :] = o_chunk.astype(o_ref.dtype)
        kd = (kc * jnp.exp(b_end - bc)).astype(BF16)
        v_t = jnp.transpose(vc.astype(F32)).astype(BF16)
        upd = jnp.dot(v_t, kd, preferred_element_type=F32)
        state = state * jnp.exp(b_end) + jnp.where(blockdiag, upd, 0.0)
    s_scr[...] = state
    sfin_ref[0] = state


def _gla_pass(pb, gz, wg_pad, bg, tri, s0, reverse, final_args=None):
    bsz, length, _ = pb.shape
    tb = min(length, 512)
    nb = length // tb
    blk = (lambda j: nb - 1 - j) if reverse else (lambda j: j)
    in_specs = [
        pl.BlockSpec((1, tb, GLA_K_WIDTH), lambda b, j: (b, blk(j), OFF_GQ // GLA_K_WIDTH)),
        pl.BlockSpec((1, tb, GLA_K_WIDTH), lambda b, j: (b, blk(j), OFF_GK // GLA_K_WIDTH)),
        pl.BlockSpec((1, tb, GLA_WIDTH), lambda b, j: (b, blk(j), OFF_GV // GLA_WIDTH)),
        pl.BlockSpec((1, tb, LANES), lambda b, j: (b, blk(j), 0)),
        pl.BlockSpec((LANES, GLA_K_WIDTH), lambda b, j: (0, 0)),
        pl.BlockSpec((1, GLA_K_WIDTH), lambda b, j: (0, 0)),
        pl.BlockSpec((tb, tb), lambda b, j: (0, 0)),
        pl.BlockSpec((1, GLA_WIDTH, GLA_K_WIDTH), lambda b, j: (b, 0, 0)),
    ]
    args = [pb, pb, pb, gz, wg_pad, bg, tri, s0]
    final = final_args is not None
    if final:
        o_prev, norm_g = final_args
        in_specs += [
            pl.BlockSpec((1, tb, GLA_WIDTH), lambda b, j: (b, blk(j), 0)),
            pl.BlockSpec((1, tb, GLA_WIDTH), lambda b, j: (b, blk(j), OFF_GR // GLA_WIDTH)),
            pl.BlockSpec((1, GLA_DV), lambda b, j: (0, 0)),
        ]
        args += [o_prev, pb, norm_g.reshape(1, GLA_DV)]
    return pl.pallas_call(
        functools.partial(_gla_kernel, reverse=reverse, final=final),
        grid=(bsz, nb),
        in_specs=in_specs,
        out_specs=[pl.BlockSpec((1, tb, GLA_WIDTH), lambda b, j: (b, blk(j), 0)),
                   pl.BlockSpec((1, GLA_WIDTH, GLA_K_WIDTH), lambda b, j: (b, 0, 0))],
        out_shape=[jax.ShapeDtypeStruct((bsz, length, GLA_WIDTH), BF16 if final else F32),
                   jax.ShapeDtypeStruct((bsz, GLA_WIDTH, GLA_K_WIDTH), F32)],
        scratch_shapes=[pltpu.VMEM((GLA_WIDTH, GLA_K_WIDTH), F32)],
        compiler_params=_cparams(("parallel", "arbitrary")),
        name="gla_bwd" if reverse else "gla_fwd",
    )(*args)


def _gla_consts(tb):
    tri = np.kron(np.eye(tb // GLA_CHUNK), np.tril(np.ones((GLA_CHUNK, GLA_CHUNK)))).astype(np.float32)
    return jnp.asarray(tri).astype(BF16), jnp.asarray(tri.T.copy()).astype(BF16)


def _slab_rows(n_slabs):
    return n_slabs * SLAB_PITCH


def _store_slabs(ref, lead, val, n_slabs):
    rows, lanes = val.shape
    for s in range(n_slabs):
        base = s * SLAB_PITCH
        lo = s * FFT_N1
        if lo + FFT_N1 <= rows:
            ref[lead + (slice(base, base + FFT_N1), slice(None))] = val[lo:lo + FFT_N1]
            ref[lead + (slice(base + FFT_N1, base + SLAB_PITCH), slice(None))] = jnp.zeros(
                (SLAB_PITCH - FFT_N1, lanes), val.dtype)
        else:
            ref[lead + (slice(base, base + SLAB_PITCH), slice(None))] = jnp.zeros((SLAB_PITCH, lanes), val.dtype)


def _hy_filter_kernel(feat_ref, w1_ref, b1_ref, f1_ref, w2_ref, b2_ref, f2_ref, w3_ref, dl_ref, o_ref, *, length):
    tl = feat_ref.shape[0]
    feat = feat_ref[...]
    h = jnp.sin(f1_ref[...] * (jnp.dot(feat, w1_ref[...], precision=HIGHEST, preferred_element_type=F32)
                               + b1_ref[...]))
    h = jnp.sin(f2_ref[...] * (jnp.dot(h, w2_ref[...], precision=HIGHEST, preferred_element_type=F32)
                               + b2_ref[...]))
    h = jnp.dot(h, w3_ref[...], precision=HIGHEST, preferred_element_type=F32)
    win = jnp.exp(-feat[:, 0:1] * dl_ref[...])
    pos = lax.broadcasted_iota(jnp.int32, win.shape, 0) + pl.program_id(0) * tl
    valid = pos < length
    taps_f = jnp.where(valid, h[:, :HY_WIDTH] * win, 0.0)
    taps_b = jnp.where(jnp.logical_and(valid, pos > 0), h[:, HY_WIDTH:] * win, 0.0)
    _store_slabs(o_ref, (0,), taps_f, tl // FFT_N1)
    _store_slabs(o_ref, (1,), taps_b, tl // FFT_N1)


def _hy_feats(length, lpad):
    t = jnp.linspace(0.0, 1.0, length, dtype=F32)[:, None]
    bands = (HY_POS_DIM - 1) // 2
    w = 2.0 * math.pi * jnp.arange(length, dtype=F32)[:, None] / length
    f = jnp.linspace(1e-4, bands - 1, bands, dtype=F32)[None, :]
    feat = jnp.concatenate([t, jnp.cos(f * w), -jnp.sin(f * w)], axis=-1)
    return jnp.zeros((lpad, LANES), F32).at[:length, :HY_POS_DIM].set(feat)


def _hy_filters(feat, length, p):
    lpad = feat.shape[0]
    tl = min(lpad, 512)
    max_decay = math.log(HY_TARGET) / HY_FAST_PCT
    min_decay = math.log(HY_TARGET) / HY_SLOW_PCT
    deltas = jnp.abs(jnp.linspace(min_decay, max_decay, HY_WIDTH, dtype=F32))[None, :]
    w1 = jnp.zeros((LANES, HY_HIDDEN), F32).at[:HY_POS_DIM].set(p['hy_w1'])
    full = lambda shape: pl.BlockSpec(shape, lambda i: (0,) * len(shape))
    tl_slab = _slab_rows(tl // FFT_N1)
    return pl.pallas_call(
        functools.partial(_hy_filter_kernel, length=length),
        grid=(lpad // tl,),
        in_specs=[pl.BlockSpec((tl, LANES), lambda i: (i, 0)),
                  full((LANES, HY_HIDDEN)), full((1, HY_HIDDEN)), full((1, HY_HIDDEN)),
                  full((HY_HIDDEN, HY_HIDDEN)), full((1, HY_HIDDEN)), full((1, HY_HIDDEN)),
                  full((HY_HIDDEN, 2 * HY_WIDTH)), full((1, HY_WIDTH))],
        out_specs=pl.BlockSpec((2, tl_slab, HY_WIDTH), lambda i: (0, i, 0)),
        out_shape=jax.ShapeDtypeStruct((2, _slab_rows(lpad // FFT_N1), HY_WIDTH), F32),
        compiler_params=_cparams(("parallel",)),
        name="hy_filter",
    )(feat, w1, p['hy_b1'].reshape(1, -1), p['hy_freq1'].reshape(1, -1), p['hy_w2'],
      p['hy_b2'].reshape(1, -1), p['hy_freq2'].reshape(1, -1), p['hy_w3'], deltas)


def _hy_pre_kernel(x0_ref, x1_ref, vv_ref, w0_ref, w1_ref, wv_ref, b0_ref, b1_ref, bv_ref, x0o_ref, u_ref, *, n_slabs):
    length = x0_ref.shape[1]
    row = lax.broadcasted_iota(jnp.int32, (length, x0_ref.shape[2]), 0)

    def conv(x_ref, w_ref, b_ref):
        x = x_ref[0].astype(F32)
        w = w_ref[...]
        xm = jnp.where(row == 0, 0.0, pltpu.roll(x, 1, axis=0))
        xp = jnp.where(row == length - 1, 0.0, pltpu.roll(x, length - 1, axis=0))
        return xm * w[0:1] + x * w[1:2] + xp * w[2:3] + b_ref[...]

    x0 = conv(x0_ref, w0_ref, b0_ref)
    u = conv(x1_ref, w1_ref, b1_ref) * conv(vv_ref, wv_ref, bv_ref)
    _store_slabs(x0o_ref, (0,), x0, n_slabs)
    _store_slabs(u_ref, (0,), u, n_slabs)


def _hy_pre(pb, conv_w, conv_b, n_slabs):
    bsz, length, _ = pb.shape
    nct = HY_WIDTH // LANES
    base = OFF_HY // LANES
    xs = lambda g: pl.BlockSpec((1, length, LANES), lambda b, j, g=g: (b, 0, base + g * nct + j))
    ws = lambda g: pl.BlockSpec((3, LANES), lambda b, j, g=g: (0, g * nct + j))
    bs = lambda g: pl.BlockSpec((1, LANES), lambda b, j, g=g: (0, g * nct + j))
    cb = conv_b.reshape(1, -1)
    rows = _slab_rows(n_slabs)
    return pl.pallas_call(
        functools.partial(_hy_pre_kernel, n_slabs=n_slabs),
        grid=(bsz, nct),
        in_specs=[xs(0), xs(1), xs(2), ws(0), ws(1), ws(2), bs(0), bs(1), bs(2)],
        out_specs=[pl.BlockSpec((1, rows, LANES), lambda b, j: (b, 0, j))] * 2,
        out_shape=[jax.ShapeDtypeStruct((bsz, rows, HY_WIDTH), F32)] * 2,
        compiler_params=_cparams(("parallel", "parallel")),
        name="hy_pre",
    )(pb, pb, pb, conv_w, conv_w, conv_w, cb, cb, cb)


def _fft_consts(n2, nin):
    n1 = FFT_N1
    n = n1 * n2
    k2 = np.arange(n2)[:, None]
    m2 = np.arange(nin)[None, :]
    ang_a = 2.0 * np.pi * (k2 * m2 % n2) / n2
    fa = np.concatenate([np.cos(ang_a), -np.sin(ang_a)], axis=0)
    ga = np.concatenate([np.cos(ang_a).T, -np.sin(ang_a).T], axis=1) / n
    k1 = np.arange(n1)[None, :, None]
    j1 = np.arange(n1)[None, None, :]
    kk2 = np.arange(n2)[:, None, None]
    ang_b = 2.0 * np.pi * ((j1 * k1 * n2 + j1 * kk2) % n) / n
    mr, mi = np.cos(ang_b), -np.sin(ang_b)
    big = np.concatenate([np.concatenate([mr, -mi], axis=2), np.concatenate([mi, mr], axis=2)], axis=1)
    big_t = np.transpose(big, (0, 2, 1))
    f32 = lambda a: jnp.asarray(a.astype(np.float32))
    bf16 = lambda a: jnp.asarray(a.astype(np.float32)).astype(BF16)
    return f32(fa), f32(ga), bf16(big), bf16(big_t)


def _fft_a_kernel(f_ref, x_ref, o_ref):
    nin = f_ref.shape[1]
    n_out = f_ref.shape[0]
    groups = x_ref.shape[0]
    f = f_ref[...]

    def body(n1, carry):
        for g in range(groups):
            xs = x_ref[g, pl.ds(n1, nin, stride=SLAB_PITCH), :]
            o_ref[g, pl.ds(n1, n_out, stride=SLAB_PITCH), :] = jnp.dot(f, xs, preferred_element_type=F32)
        return carry

    lax.fori_loop(0, FFT_N1, body, 0, unroll=8)
    zeros = jnp.zeros((n_out, x_ref.shape[2]), F32)
    for g in range(groups):
        for r in range(FFT_N1, SLAB_PITCH):
            o_ref[g, pl.ds(r, n_out, stride=SLAB_PITCH), :] = zeros


def _fft_stage_a(x, fa):
    g, rows, ch = x.shape
    n_out, nin = fa.shape
    assert rows == _slab_rows(nin)
    return pl.pallas_call(
        _fft_a_kernel,
        grid=(g, ch // LANES),
        in_specs=[pl.BlockSpec(fa.shape, lambda b, j: (0, 0)),
                  pl.BlockSpec((1, rows, LANES), lambda b, j: (b, 0, j))],
        out_specs=pl.BlockSpec((1, _slab_rows(n_out), LANES), lambda b, j: (b, 0, j)),
        out_shape=jax.ShapeDtypeStruct((g, _slab_rows(n_out), ch), F32),
        compiler_params=_cparams(("parallel", "parallel")),
        name="fft_stage_a",
    )(fa, x)


def _fft_spec_kernel(m_ref, a_ref, o_ref):
    big = m_ref[0]
    xf = jnp.dot(big, jnp.concatenate([a_ref[0, 0, 0], a_ref[0, 1, 0]], axis=0).astype(BF16),
                 preferred_element_type=F32)
    xb = jnp.dot(big, jnp.concatenate([a_ref[1, 0, 0], a_ref[1, 1, 0]], axis=0).astype(BF16),
                 preferred_element_type=F32)
    o_ref[0, 0] = xf[:FFT_N1] + xb[:FFT_N1]
    o_ref[1, 0] = xf[FFT_N1:] - xb[FFT_N1:]


def _filter_spectrum(a_taps, big, n2):
    ch = a_taps.shape[-1]
    av = a_taps.reshape(2, 2, n2, SLAB_PITCH, ch)
    return pl.pallas_call(
        _fft_spec_kernel,
        grid=(n2,),
        in_specs=[pl.BlockSpec((1, 2 * FFT_N1, 2 * FFT_N1), lambda k: (k, 0, 0)),
                  pl.BlockSpec((2, 2, 1, FFT_N1, ch), lambda k: (0, 0, k, 0, 0))],
        out_specs=pl.BlockSpec((2, 1, FFT_N1, ch), lambda k: (0, k, 0, 0)),
        out_shape=jax.ShapeDtypeStruct((2, n2, FFT_N1, ch), F32),
        compiler_params=_cparams(("parallel",)),
        name="fft_filter_spec",
    )(big, av)


def _fft_conv_kernel(m_ref, mt_ref, h_ref, a_ref, o_ref):
    hr, hi = h_ref[0, 0], h_ref[1, 0]
    for b in range(a_ref.shape[0]):
        x = jnp.dot(m_ref[0], jnp.concatenate([a_ref[b, 0, 0], a_ref[b, 1, 0]], axis=0).astype(BF16),
                    preferred_element_type=F32)
        xr, xi = x[:FFT_N1], x[FFT_N1:]
        y = jnp.concatenate([xr * hr - xi * hi, xr * hi + xi * hr], axis=0).astype(BF16)
        t = jnp.dot(mt_ref[0], y, preferred_element_type=F32)
        pad = jnp.zeros((SLAB_PITCH - FFT_N1, t.shape[1]), F32)
        o_ref[b, 0, 0, 0:FFT_N1, :] = t[:FFT_N1]
        o_ref[b, 1, 0, 0:FFT_N1, :] = t[FFT_N1:]
        o_ref[b, 0, 0, FFT_N1:SLAB_PITCH, :] = pad
        o_ref[b, 1, 0, FFT_N1:SLAB_PITCH, :] = pad


def _fft_conv_mid(a_sig, spec, big, big_t, n2):
    bsz = a_sig.shape[0]
    ch = a_sig.shape[-1]
    av = a_sig.reshape(bsz, 2, n2, SLAB_PITCH, ch)
    out = pl.pallas_call(
        _fft_conv_kernel,
        grid=(n2,),
        in_specs=[pl.BlockSpec((1, 2 * FFT_N1, 2 * FFT_N1), lambda k: (k, 0, 0)),
                  pl.BlockSpec((1, 2 * FFT_N1, 2 * FFT_N1), lambda k: (k, 0, 0)),
                  pl.BlockSpec((2, 1, FFT_N1, ch), lambda k: (0, k, 0, 0)),
                  pl.BlockSpec((bsz, 2, 1, FFT_N1, ch), lambda k: (0, 0, k, 0, 0))],
        out_specs=pl.BlockSpec((bsz, 2, 1, SLAB_PITCH, ch), lambda k: (0, 0, k, 0, 0)),
        out_shape=jax.ShapeDtypeStruct((bsz, 2, n2, SLAB_PITCH, ch), F32),
        compiler_params=_cparams(("parallel",)),
        name="fft_conv_mid",
    )(big, big_t, spec, av)
    return out.reshape(bsz, 2 * n2 * SLAB_PITCH, ch)


def _fft_out_kernel(g_ref, t_ref, x0_ref, u_ref, skip_ref, o_ref):
    nin, n_in = g_ref.shape
    g = g_ref[...]
    skip = skip_ref[...]

    def body(n1, carry):
        ts = t_ref[0, pl.ds(n1, n_in, stride=SLAB_PITCH), :]
        y = jnp.dot(g, ts, preferred_element_type=F32)
        x0 = x0_ref[0, pl.ds(n1, nin, stride=SLAB_PITCH), :]
        u = u_ref[0, pl.ds(n1, nin, stride=SLAB_PITCH), :]
        o_ref[0, pl.ds(n1, nin, stride=FFT_N1), :] = x0 * (y + skip * u)
        return carry

    lax.fori_loop(0, FFT_N1, body, 0, unroll=8)


def _fft_out(t, ga, x0, u, skip):
    bsz, rows, ch = u.shape
    nin, n_in = ga.shape
    return pl.pallas_call(
        _fft_out_kernel,
        grid=(bsz, ch // LANES),
        in_specs=[pl.BlockSpec(ga.shape, lambda b, j: (0, 0)),
                  pl.BlockSpec((1, _slab_rows(n_in), LANES), lambda b, j: (b, 0, j)),
                  pl.BlockSpec((1, rows, LANES), lambda b, j: (b, 0, j)),
                  pl.BlockSpec((1, rows, LANES), lambda b, j: (b, 0, j)),
                  pl.BlockSpec((1, LANES), lambda b, j: (0, j))],
        out_specs=pl.BlockSpec((1, nin * FFT_N1, LANES), lambda b, j: (b, 0, j)),
        out_shape=jax.ShapeDtypeStruct((bsz, nin * FFT_N1, ch), F32),
        compiler_params=_cparams(("parallel", "parallel")),
        name="fft_out",
    )(ga, t, x0, u, skip.reshape(1, ch))


def _hyena(pb, p, consts, length):
    fa, ga, big, big_t, feat, n2, nin = consts
    taps = _hy_filters(feat, length, p)
    spec = _filter_spectrum(_fft_stage_a(taps, fa), big, n2)
    x0, u = _hy_pre(pb, p['hy_conv_w'], p['hy_conv_b'], nin)
    t = _fft_conv_mid(_fft_stage_a(u, fa), spec, big, big_t, n2)
    y = _fft_out(t, ga, x0, u, p['hy_skip'])
    return y if y.shape[1] == length else y[:, :length]


def _hyena_consts(length):
    if length >= 2048:
        lpad, n2 = length, 2 * length // FFT_N1
    else:
        lpad = max(4 * length, 1024)
        n2 = lpad // FFT_N1
    nin = lpad // FFT_N1
    return _fft_consts(n2, nin) + (_hy_feats(length, lpad), n2, nin)


def _merge_kernel(ya_ref, yb_ref, yc_ref, lg_ref, x_ref, gate_ref, lng_ref, lnb_ref, wb_ref, wo_ref, o_ref):
    lg = lg_ref[0].astype(F32)
    ys = (ya_ref[0], yb_ref[0], yc_ref[0].astype(BF16))
    mix = None
    for g in range(N_BRANCH):
        proj = jnp.dot(ys[g], wb_ref[g], preferred_element_type=F32)
        term = jax.nn.sigmoid(lg[:, g * D_MODEL:(g + 1) * D_MODEL]) * proj
        mix = term if mix is None else mix + term
    y = jnp.dot(mix.astype(BF16), wo_ref[...], preferred_element_type=F32)
    z = DEEPNORM_ALPHA * x_ref[0] + gate_ref[0] * y
    o_ref[0] = _layer_norm_f32(z) * lng_ref[...] + lnb_ref[...]


def _merge(ya, yb, yc, pb, x, gate, ln_g, ln_b, w_branch, w_out):
    bsz, length, d = x.shape
    tm = min(length, 512)
    return pl.pallas_call(
        _merge_kernel,
        grid=(bsz, length // tm),
        in_specs=[pl.BlockSpec((1, tm, 512), lambda b, i: (b, i, 0)),
                  pl.BlockSpec((1, tm, 512), lambda b, i: (b, i, 0)),
                  pl.BlockSpec((1, tm, 512), lambda b, i: (b, i, 0)),
                  pl.BlockSpec((1, tm, N_BRANCH * d), lambda b, i: (b, i, OFF_GATE // (N_BRANCH * d))),
                  pl.BlockSpec((1, tm, d), lambda b, i: (b, i, 0)),
                  pl.BlockSpec((1, 1, d), lambda b, i: (b, 0, 0)),
                  pl.BlockSpec((1, d), lambda b, i: (0, 0)),
                  pl.BlockSpec((1, d), lambda b, i: (0, 0)),
                  pl.BlockSpec((N_BRANCH, 512, d), lambda b, i: (0, 0, 0)),
                  pl.BlockSpec((d, d), lambda b, i: (0, 0))],
        out_specs=pl.BlockSpec((1, tm, d), lambda b, i: (b, i, 0)),
        out_shape=jax.ShapeDtypeStruct((bsz, length, d), F32),
        compiler_params=_cparams(("parallel", "parallel")),
        name="merge",
    )(ya, yb, yc, pb, x, gate, ln_g.reshape(1, d), ln_b.reshape(1, d), w_branch, w_out)


FF_CHUNK = 256


def _gelu_tanh(x):
    return 0.5 * x * (1.0 + jnp.tanh(math.sqrt(2.0 / math.pi) * (x + 0.044715 * (x * x * x))))


def _ffn_kernel(xp_ref, x_ref, xn_ref, sh_ref, sc_ref, gate_ref, lng_ref, lnb_ref,
                wup_ref, cw_ref, cb_ref, wdn_ref, o_ref, h_scr, acc_scr):
    i = pl.program_id(1)
    nt = pl.num_programs(1)
    tm = x_ref.shape[1]
    sc, sh = 1.0 + sc_ref[0], sh_ref[0]
    x = x_ref[0]
    h_scr[0:8, :] = _layer_norm_f32(xp_ref[0]) * sc + sh
    h_scr[8:8 + tm, :] = _layer_norm_f32(x) * sc + sh
    h_scr[8 + tm:16 + tm, :] = _layer_norm_f32(xn_ref[0]) * sc + sh
    hext = h_scr[...].astype(BF16)

    row = lax.broadcasted_iota(jnp.int32, (tm, FF_CHUNK), 0)
    keep_prev = jnp.logical_or(row > 0, i > 0)
    keep_next = jnp.logical_or(row < tm - 1, i < nt - 1)

    def conv(up, w, b):
        um = jnp.where(keep_prev, up[7:7 + tm], 0.0)
        un = jnp.where(keep_next, up[9:9 + tm], 0.0)
        return um * w[0:1] + up[8:8 + tm] * w[1:2] + un * w[2:3] + b

    n_chunks = D_FF // FF_CHUNK
    for cidx in range(n_chunks):
        ca = cidx * FF_CHUNK
        cg = D_FF + cidx * FF_CHUNK
        up_a = jnp.dot(hext, wup_ref[:, ca:ca + FF_CHUNK], preferred_element_type=F32)
        up_g = jnp.dot(hext, wup_ref[:, cg:cg + FF_CHUNK], preferred_element_type=F32)
        a = conv(up_a, cw_ref[:, ca:ca + FF_CHUNK], cb_ref[:, ca:ca + FF_CHUNK])
        g = conv(up_g, cw_ref[:, cg:cg + FF_CHUNK], cb_ref[:, cg:cg + FF_CHUNK])
        act = (_gelu_tanh(g) * a).astype(BF16)
        part = jnp.dot(act, wdn_ref[ca:ca + FF_CHUNK, :], preferred_element_type=F32)
        if cidx == 0:
            acc_scr[...] = part
        else:
            acc_scr[...] += part
    z = DEEPNORM_ALPHA * x + gate_ref[0] * acc_scr[...]
    o_ref[0] = _layer_norm_f32(z) * lng_ref[...] + lnb_ref[...]


def _ffn(x, shift, scale, gate, ln_g, ln_b, w_up, conv_w, conv_b, w_down):
    bsz, length, d = x.shape
    tm = min(length, 512)
    nt = length // tm
    r8 = tm // 8
    last8 = length // 8 - 1
    vec = pl.BlockSpec((1, 1, d), lambda b, i: (b, 0, 0))
    row = pl.BlockSpec((1, d), lambda b, i: (0, 0))
    return pl.pallas_call(
        _ffn_kernel,
        grid=(bsz, nt),
        in_specs=[pl.BlockSpec((1, 8, d), lambda b, i: (b, jnp.maximum(i * r8 - 1, 0), 0)),
                  pl.BlockSpec((1, tm, d), lambda b, i: (b, i, 0)),
                  pl.BlockSpec((1, 8, d), lambda b, i: (b, jnp.minimum((i + 1) * r8, last8), 0)),
                  vec, vec, vec, row, row,
                  pl.BlockSpec((d, 2 * D_FF), lambda b, i: (0, 0)),
                  pl.BlockSpec((3, 2 * D_FF), lambda b, i: (0, 0)),
                  pl.BlockSpec((1, 2 * D_FF), lambda b, i: (0, 0)),
                  pl.BlockSpec((D_FF, d), lambda b, i: (0, 0))],
        out_specs=pl.BlockSpec((1, tm, d), lambda b, i: (b, i, 0)),
        out_shape=jax.ShapeDtypeStruct((bsz, length, d), F32),
        scratch_shapes=[pltpu.VMEM((tm + 16, d), F32), pltpu.VMEM((tm, d), F32)],
        compiler_params=_cparams(("parallel", "parallel")),
        name="conv_ffn",
    )(x, x, x, shift, scale, gate, ln_g.reshape(1, d), ln_b.reshape(1, d),
      w_up, conv_w, conv_b.reshape(1, -1), w_down)


def _prep_w_in(w_in):
    a_q, a_k, a_v, g_q, g_k, g_v, g_r, g_z, hy, gate = jnp.split(
        w_in, [512, 1024, 1536, 1792, 2048, 2560, 3072, 3104, 4640], axis=-1)
    wb = jnp.concatenate([a_q, a_k, a_v, g_q, g_k, g_v, g_r, gate, hy], axis=-1).astype(BF16)
    wz = jnp.concatenate([g_z, jnp.zeros((w_in.shape[0], LANES - 2 * GLA_GATE_RANK), w_in.dtype)],
                         axis=-1).astype(BF16)
    return wb, wz


def _prep_gate(w_gate, b_gate):
    pads = []
    for d in range(2):
        m = jnp.zeros((LANES, GLA_K_WIDTH), F32).at[d * GLA_GATE_RANK:(d + 1) * GLA_GATE_RANK].set(w_gate[d])
        pads.append(m)
    return pads, [b_gate[0].reshape(1, -1), b_gate[1].reshape(1, -1)]


def kernel(x, c, ctx, c_ctx, w_ada, b_ada, w_in, da_lambda, da_norm_g, gla_w_gate, gla_b_gate, gla_norm_g,
           hy_conv_w, hy_conv_b, hy_w1, hy_b1, hy_freq1, hy_w2, hy_b2, hy_freq2, hy_w3, hy_skip, w_branch,
           w_out, ln1_g, ln1_b, ffn_w_up, ffn_conv_w, ffn_conv_b, ffn_w_down, ln2_g, ln2_b):
    bsz, length, d = x.shape
    c_len = ctx.shape[1]
    depth = w_ada.shape[0]

    cc = jnp.zeros((8, d), F32).at[:bsz].set(c).at[bsz].set(c_ctx)
    mod = _ada_all(cc, w_ada, b_ada)

    rope = _rope_tables(length)
    hy_lat = _hyena_consts(length)
    hy_ctx = _hyena_consts(c_len)
    tri_lat = _gla_consts(min(length, 512))
    tri_ctx = _gla_consts(min(c_len, 512))
    zero_state = jnp.zeros((bsz, GLA_WIDTH, GLA_K_WIDTH), F32)

    x_lat, x_ctx = x, ctx
    for l in range(depth):
        with_ctx = l < depth - 1
        lam_init = 0.8 - 0.6 * math.exp(-0.3 * l)
        p = {'hy_conv_w': hy_conv_w[l], 'hy_conv_b': hy_conv_b[l], 'hy_w1': hy_w1[l], 'hy_b1': hy_b1[l],
             'hy_freq1': hy_freq1[l], 'hy_w2': hy_w2[l], 'hy_b2': hy_b2[l], 'hy_freq2': hy_freq2[l],
             'hy_w3': hy_w3[l], 'hy_skip': hy_skip[l]}
        wb, wz = _prep_w_in(w_in[l])
        wg_pad, bg = _prep_gate(gla_w_gate[l], gla_b_gate[l])
        wbr = w_branch[l].astype(BF16)
        wo = w_out[l].astype(BF16)
        wup = ffn_w_up[l].astype(BF16)
        wdn = ffn_w_down[l].astype(BF16)

        m_lat = mod[l, :bsz].reshape(bsz, 1, 6 * d)
        m_ctx = jnp.broadcast_to(mod[l, bsz].reshape(1, 1, 6 * d), (bsz, 1, 6 * d))
        sh1, sc1, g1, sh2, sc2, g2 = [m_lat[..., k * d:(k + 1) * d] for k in range(6)]
        csh1, csc1, cg1, csh2, csc2, cg2 = [m_ctx[..., k * d:(k + 1) * d] for k in range(6)]

        pb, gz = _inproj(x_lat, sh1, sc1, wb, wz)
        cpb, cgz = _inproj(x_ctx, csh1, csc1, wb, wz)

        y_a = _diff_attention(pb, [cpb, pb], da_lambda[l], da_norm_g[l], lam_init, rope=rope)
        co_f, cs_f = _gla_pass(cpb, cgz, wg_pad[0], bg[0], tri_ctx[0], zero_state, False)
        if with_ctx:
            y_cb, cs_b = _gla_pass(cpb, cgz, wg_pad[1], bg[1], tri_ctx[1], zero_state, True,
                                   final_args=(co_f, gla_norm_g[l]))
        else:
            _, cs_b = _gla_pass(cpb, cgz, wg_pad[1], bg[1], tri_ctx[1], zero_state, True)
        o_f, _ = _gla_pass(pb, gz, wg_pad[0], bg[0], tri_lat[0], cs_f, False)
        y_b, _ = _gla_pass(pb, gz, wg_pad[1], bg[1], tri_lat[1], cs_b, True, final_args=(o_f, gla_norm_g[l]))
        y_c = _hyena(pb, p, hy_lat, length)

        x_lat = _merge(y_a, y_b, y_c, pb, x_lat, g1, ln1_g[l], ln1_b[l], wbr, wo)
        x_lat = _ffn(x_lat, sh2, sc2, g2, ln2_g[l], ln2_b[l], wup, ffn_conv_w[l], ffn_conv_b[l], wdn)
        if with_ctx:
            y_ca = _diff_attention(cpb, [cpb], da_lambda[l], da_norm_g[l], lam_init)
            y_cc = _hyena(cpb, p, hy_ctx, c_len)
            x_ctx = _merge(y_ca, y_cb, y_cc, cpb, x_ctx, cg1, ln1_g[l], ln1_b[l], wbr, wo)
            x_ctx = _ffn(x_ctx, csh2, csc2, cg2, ln2_g[l], ln2_b[l], wup, ffn_conv_w[l], ffn_conv_b[l], wdn)
    return x_lat
```

```python
import functools
import math

import numpy as np
import jax
import jax.numpy as jnp
from jax import lax
from jax.experimental import pallas as pl
from jax.experimental.pallas import tpu as pltpu

F32 = jnp.float32
BF16 = jnp.bfloat16
HIGHEST = lax.Precision.HIGHEST

D_MODEL = 1024
DEPTH = 4
GRID_W = 64
DA_HEADS = 4
DA_HEAD_DIM = 64
DA_V_DIM = 128
ROPE_THETA = 10000.0
GLA_HEADS = 4
GLA_DK = 64
GLA_DV = 128
GLA_K_WIDTH = 256
GLA_WIDTH = 512
GLA_GATE_RANK = 16
GLA_GATE_TAU = 16.0
GLA_CHUNK = 64
HY_WIDTH = 512
HY_POS_DIM = 33
HY_HIDDEN = 64
HY_TARGET = 1e-2
HY_FAST_PCT = 0.3
HY_SLOW_PCT = 1.5
N_BRANCH = 3
D_FF = 2816
LN_EPS = 1e-5
DEEPNORM_ALPHA = (2.0 * DEPTH) ** 0.25

NB_COLS = 7680
OFF_AQ, OFF_AK, OFF_AV, OFF_GQ, OFF_GK, OFF_GV, OFF_GR, OFF_GATE, OFF_HY = (
    0, 512, 1024, 1536, 1792, 2048, 2560, 3072, 6144)

LANES = 128
FFT_N1 = 128
SLAB_PITCH = 136
VMEM_LIMIT = 56 * 1024 * 1024


def _cparams(sem):
    return pltpu.CompilerParams(dimension_semantics=sem, vmem_limit_bytes=VMEM_LIMIT)


def _layer_norm_f32(x):
    mu = jnp.mean(x, axis=-1, keepdims=True)
    xc = x - mu
    var = jnp.mean(xc * xc, axis=-1, keepdims=True)
    return xc * lax.rsqrt(var + LN_EPS)


def _ada_kernel(c_ref, w_ref, b_ref, o_ref):
    cc = c_ref[...]
    s = cc * jax.nn.sigmoid(cc)
    o_ref[0] = jnp.dot(s, w_ref[0], precision=HIGHEST, preferred_element_type=F32) + b_ref[0]


def _ada_all(cc, w_ada, b_ada):
    depth, d, n = w_ada.shape
    tn = 1536
    return pl.pallas_call(
        _ada_kernel,
        grid=(depth, n // tn),
        in_specs=[
            pl.BlockSpec((8, d), lambda l, j: (0, 0)),
            pl.BlockSpec((1, d, tn), lambda l, j: (l, 0, j)),
            pl.BlockSpec((1, 1, tn), lambda l, j: (l, 0, j)),
        ],
        out_specs=pl.BlockSpec((1, 8, tn), lambda l, j: (l, 0, j)),
        out_shape=jax.ShapeDtypeStruct((depth, 8, n), F32),
        compiler_params=_cparams(("parallel", "parallel")),
        name="ada",
    )(cc, w_ada, b_ada.reshape(depth, 1, n))


def _inproj_kernel(x_ref, sh_ref, sc_ref, w_ref, wz_ref, o_ref, z_ref, h_scr):
    @pl.when(pl.program_id(2) == 0)
    def _():
        h = (_layer_norm_f32(x_ref[0]) * (1.0 + sc_ref[0]) + sh_ref[0]).astype(BF16)
        h_scr[...] = h
        z_ref[0] = jnp.dot(h, wz_ref[...], preferred_element_type=F32)

    o_ref[0] = jnp.dot(h_scr[...], w_ref[...], preferred_element_type=F32).astype(o_ref.dtype)


def _inproj(x, shift, scale, w, wz):
    bsz, length, d = x.shape
    n_cols = w.shape[1]
    tm = min(length, 1024)
    tn = 1536
    return pl.pallas_call(
        _inproj_kernel,
        grid=(bsz, length // tm, n_cols // tn),
        in_specs=[
            pl.BlockSpec((1, tm, d), lambda b, i, j: (b, i, 0)),
            pl.BlockSpec((1, 1, d), lambda b, i, j: (b, 0, 0)),
            pl.BlockSpec((1, 1, d), lambda b, i, j: (b, 0, 0)),
            pl.BlockSpec((d, tn), lambda b, i, j: (0, j)),
            pl.BlockSpec((d, LANES), lambda b, i, j: (0, 0)),
        ],
        out_specs=[pl.BlockSpec((1, tm, tn), lambda b, i, j: (b, i, j)),
                   pl.BlockSpec((1, tm, LANES), lambda b, i, j: (b, i, 0))],
        out_shape=[jax.ShapeDtypeStruct((bsz, length, n_cols), BF16),
                   jax.ShapeDtypeStruct((bsz, length, LANES), F32)],
        scratch_shapes=[pltpu.VMEM((tm, d), BF16)],
        compiler_params=_cparams(("parallel", "parallel", "arbitrary")),
        name="inproj",
    )(x, shift, scale, w, wz)


def _rope_tables(length):
    n_freq = DA_HEAD_DIM // 4
    inv = ROPE_THETA ** (-jnp.arange(n_freq, dtype=F32) / n_freq)
    rows = jnp.repeat(jnp.arange(length // GRID_W), GRID_W).astype(F32)
    cols = (jnp.arange(length) % GRID_W).astype(F32)
    ang_r = rows[:, None] * inv
    ang_c = cols[:, None] * inv
    cos64 = jnp.concatenate([jnp.cos(ang_r), jnp.cos(ang_r), jnp.cos(ang_c), jnp.cos(ang_c)], axis=-1)
    sin64 = jnp.concatenate([-jnp.sin(ang_r), jnp.sin(ang_r), -jnp.sin(ang_c), jnp.sin(ang_c)], axis=-1)
    return jnp.concatenate([cos64, cos64], axis=-1), jnp.concatenate([sin64, sin64], axis=-1)


def _swap16(x):
    n = x.shape[-1]
    lane = lax.broadcasted_iota(jnp.int32, x.shape, x.ndim - 1)
    up = pltpu.roll(x, n - 16, axis=x.ndim - 1)
    dn = pltpu.roll(x, 16, axis=x.ndim - 1)
    return jnp.where((lane & 16) == 0, up, dn)


def _rope(x, cos, sin):
    return x * cos + _swap16(x) * sin


ATTN_TK = 512
ROPE_ROWS = 512


def _attn_kernel(*refs, n_src, rope, lam_init):
    lam_ref, g_ref, q_ref = refs[:3]
    pos = 3
    if rope:
        qcos_ref, qsin_ref, kcos_ref, ksin_ref = refs[pos:pos + 4]
        pos += 4
    kv_refs = refs[pos:pos + 2 * n_src]
    o_ref = refs[pos + 2 * n_src]
    scr = refs[pos + 2 * n_src + 1:]
    vx_scr = scr[:n_src]
    krot_scr = scr[n_src] if rope else None
    tq = q_ref.shape[1]

    @pl.when(pl.program_id(2) == 0)
    def _():
        for s_idx in range(n_src):
            v_ref = kv_refs[2 * s_idx + 1]
            n_keys = v_ref.shape[1]
            vx_scr[s_idx][:, 0:DA_V_DIM] = v_ref[0]
            vx_scr[s_idx][:, DA_V_DIM:2 * DA_V_DIM] = jnp.ones((n_keys, DA_V_DIM), BF16)
        if rope:
            k_ref = kv_refs[2 * (n_src - 1)]
            rr = min(ROPE_ROWS, k_ref.shape[1])

            def rot(j, carry):
                r0 = pl.multiple_of(j * rr, rr)
                kk = k_ref[0, pl.ds(r0, rr), :].astype(F32)
                krot_scr[pl.ds(r0, rr), :] = _rope(kk, kcos_ref[pl.ds(r0, rr), :],
                                                   ksin_ref[pl.ds(r0, rr), :]).astype(BF16)
                return carry

            lax.fori_loop(0, k_ref.shape[1] // rr, rot, 0)

    lp = lam_ref[...]
    lam = (jnp.exp(jnp.sum(lp[0] * lp[1], axis=-1, keepdims=True))
           - jnp.exp(jnp.sum(lp[2] * lp[3], axis=-1, keepdims=True)) + lam_init)

    q = q_ref[0].astype(F32)
    if rope:
        q = _rope(q, qcos_ref[...], qsin_ref[...])
    q = (q * (DA_HEAD_DIM ** -0.5 * math.log2(math.e))).astype(BF16)
    lane = lax.broadcasted_iota(jnp.int32, q.shape, 1)
    zero = jnp.zeros_like(q)
    qs = jnp.concatenate([jnp.where(lane < DA_HEAD_DIM, q, zero),
                          jnp.where(lane >= DA_HEAD_DIM, q, zero)], axis=0)

    m = jnp.full((2 * tq, 1), -1e30, F32)
    acc = jnp.zeros((2 * tq, 2 * DA_V_DIM), F32)
    for s_idx in range(n_src):
        k_ref = kv_refs[2 * s_idx]
        k_src = krot_scr if (rope and s_idx == n_src - 1) else k_ref.at[0]
        n_keys = k_ref.shape[1]
        tk = min(ATTN_TK, n_keys)
        for j in range(n_keys // tk):
            kb = k_src[j * tk:(j + 1) * tk, :]
            vb = vx_scr[s_idx][j * tk:(j + 1) * tk, :]
            s = lax.dot_general(qs, kb, (((1,), (1,)), ((), ())), preferred_element_type=F32)
            m_new = jnp.maximum(m, jnp.max(s, axis=-1, keepdims=True))
            p = jnp.exp2(s - m_new).astype(BF16)
            acc = jnp.exp2(m - m_new) * acc + jnp.dot(p, vb, preferred_element_type=F32)
            m = m_new

    o = acc[:, :DA_V_DIM] / acc[:, DA_V_DIM:]
    a = o[:tq] - lam * o[tq:]
    y = a * lax.rsqrt(jnp.mean(a * a, axis=-1, keepdims=True) + LN_EPS) * g_ref[...]
    o_ref[0] = (y * (1.0 - lam_init)).astype(o_ref.dtype)


def _diff_attention(q_arr, kv_arrs, lam_p, norm_g, lam_init, rope=None):
    bsz, lq, _ = q_arr.shape
    tq = min(lq, 512)
    in_specs = [
        pl.BlockSpec((4, 1, DA_HEAD_DIM), lambda b, h, i: (0, 0, 0)),
        pl.BlockSpec((1, DA_V_DIM), lambda b, h, i: (0, 0)),
        pl.BlockSpec((1, tq, LANES), lambda b, h, i: (b, i, OFF_AQ // LANES + h)),
    ]
    args = [lam_p.reshape(4, 1, DA_HEAD_DIM), norm_g.reshape(1, DA_V_DIM), q_arr]
    if rope is not None:
        cos, sin = rope
        in_specs += [pl.BlockSpec((tq, LANES), lambda b, h, i: (i, 0))] * 2
        in_specs += [pl.BlockSpec((lq, LANES), lambda b, h, i: (0, 0))] * 2
        args += [cos, sin, cos, sin]
    scratch = []
    for arr in kv_arrs:
        lk = arr.shape[1]
        in_specs += [pl.BlockSpec((1, lk, LANES), lambda b, h, i: (b, 0, OFF_AK // LANES + h)),
                     pl.BlockSpec((1, lk, LANES), lambda b, h, i: (b, 0, OFF_AV // LANES + h))]
        args += [arr, arr]
        scratch.append(pltpu.VMEM((lk, 2 * DA_V_DIM), BF16))
    if rope is not None:
        scratch.append(pltpu.VMEM((kv_arrs[-1].shape[1], LANES), BF16))
    return pl.pallas_call(
        functools.partial(_attn_kernel, n_src=len(kv_arrs), rope=rope is not None, lam_init=lam_init),
        grid=(bsz, DA_HEADS, lq // tq),
        in_specs=in_specs,
        out_specs=pl.BlockSpec((1, tq, LANES), lambda b, h, i: (b, i, h)),
        out_shape=jax.ShapeDtypeStruct((bsz, lq, DA_HEADS * DA_V_DIM), BF16),
        scratch_shapes=scratch,
        compiler_params=_cparams(("parallel", "parallel", "arbitrary")),
        name="diff_attn",
    )(*args)


def _gla_kernel(*refs, reverse, final):
    if final:
        (q_ref, k_ref, v_ref, z_ref, wg_ref, bg_ref, tri_ref, s0_ref, oprev_ref, r_ref, g_ref,
         o_ref, sfin_ref, s_scr) = refs
    else:
        (q_ref, k_ref, v_ref, z_ref, wg_ref, bg_ref, tri_ref, s0_ref,
         o_ref, sfin_ref, s_scr) = refs
    j = pl.program_id(1)
    tb = q_ref.shape[1]
    nch = tb // GLA_CHUNK
    c = GLA_CHUNK

    @pl.when(j == 0)
    def _():
        s_scr[...] = s0_ref[0]

    pre = jnp.dot(z_ref[0], wg_ref[...], precision=HIGHEST, preferred_element_type=F32) + bg_ref[...]
    la = (jnp.minimum(pre, 0.0) - jnp.log1p(jnp.exp(-jnp.abs(pre)))) * (1.0 / GLA_GATE_TAU)
    la_hi = la.astype(BF16)
    la_lo = (la - la_hi.astype(F32)).astype(BF16)
    bsum = jnp.dot(tri_ref[...], jnp.concatenate([la_hi, la_lo], axis=-1), preferred_element_type=F32)
    b_all = bsum[:, :GLA_K_WIDTH] + bsum[:, GLA_K_WIDTH:]

    row = lax.broadcasted_iota(jnp.int32, (GLA_HEADS * c, c), 0) % c
    col = lax.broadcasted_iota(jnp.int32, (GLA_HEADS * c, c), 1)
    causal = (row <= col) if reverse else (row >= col)
    hrow = lax.broadcasted_iota(jnp.int32, (GLA_WIDTH, GLA_K_WIDTH), 0) // GLA_DV
    hcol = lax.broadcasted_iota(jnp.int32, (GLA_WIDTH, GLA_K_WIDTH), 1) // GLA_DK
    blockdiag = hrow == hcol
    qlane_head = lax.broadcasted_iota(jnp.int32, (c, GLA_K_WIDTH), 1) // GLA_DK

    state = s_scr[...]
    order = range(nch - 1, -1, -1) if reverse else range(nch)
    for ci in order:
        r0 = ci * c
        bc = b_all[r0:r0 + c]
        if reverse:
            b_end, b_ref_row = bc[0:1], bc[c - 1 - c // 2:c - c // 2]
        else:
            b_end, b_ref_row = bc[c - 1:c], bc[c // 2:c // 2 + 1]
        qc = q_ref[0, r0:r0 + c, :].astype(F32) * (GLA_DK ** -0.5)
        kc = k_ref[0, r0:r0 + c, :].astype(F32)
        vc = v_ref[0, r0:r0 + c, :]
        qe = qc * jnp.exp(bc - b_ref_row)
        ke = (kc * jnp.exp(b_ref_row - bc)).astype(BF16)
        zq = jnp.zeros_like(qe)
        q_stack = jnp.concatenate([jnp.where(qlane_head == h, qe, zq) for h in range(GLA_HEADS)],
                                  axis=0).astype(BF16)
        att = lax.dot_general(q_stack, ke, (((1,), (1,)), ((), ())), preferred_element_type=F32)
        att = jnp.where(causal, att, 0.0).astype(BF16)
        o_intra = jnp.concatenate(
            [jnp.dot(att[h * c:(h + 1) * c], vc[:, h * GLA_DV:(h + 1) * GLA_DV], preferred_element_type=F32)
             for h in range(GLA_HEADS)], axis=-1)
        o_inter = lax.dot_general((qc * jnp.exp(bc)).astype(BF16), state.astype(BF16),
                                  (((1,), (1,)), ((), ())), preferred_element_type=F32)
        o_chunk = o_intra + o_inter
        if final:
            o_chunk = o_chunk + oprev_ref[0, r0:r0 + c, :]
            parts = []
            for h in range(GLA_HEADS):
                oh = o_chunk[:, h * GLA_DV:(h + 1) * GLA_DV]
                parts.append(oh * lax.rsqrt(jnp.mean(oh * oh, axis=-1, keepdims=True) + LN_EPS) * g_ref[...])
            rr = r_ref[0, r0:r0 + c, :].astype(F32)
            o_ref[0, r0:r0 + c, :] = (jnp.concatenate(parts, axis=-1)
                                      * (rr * jax.nn.sigmoid(rr))).astype(o_ref.dtype)
        else:
            o_ref[0, r0:r0 + c, :] = o_chunk.astype(o_ref.dtype)
        kd = (kc * jnp.exp(b_end - bc)).astype(BF16)
        v_t = jnp.transpose(vc.astype(F32)).astype(BF16)
        upd = jnp.dot(v_t, kd, preferred_element_type=F32)
        state = state * jnp.exp(b_end) + jnp.where(blockdiag, upd, 0.0)
    s_scr[...] = state
    sfin_ref[0] = state


def _gla_pass(pb, gz, wg_pad, bg, tri, s0, reverse, final_args=None):
    bsz, length, _ = pb.shape
    tb = min(length, 512)
    nb = length // tb
    blk = (lambda j: nb - 1 - j) if reverse else (lambda j: j)
    in_specs = [
        pl.BlockSpec((1, tb, GLA_K_WIDTH), lambda b, j: (b, blk(j), OFF_GQ // GLA_K_WIDTH)),
        pl.BlockSpec((1, tb, GLA_K_WIDTH), lambda b, j: (b, blk(j), OFF_GK // GLA_K_WIDTH)),
        pl.BlockSpec((1, tb, GLA_WIDTH), lambda b, j: (b, blk(j), OFF_GV // GLA_WIDTH)),
        pl.BlockSpec((1, tb, LANES), lambda b, j: (b, blk(j), 0)),
        pl.BlockSpec((LANES, GLA_K_WIDTH), lambda b, j: (0, 0)),
        pl.BlockSpec((1, GLA_K_WIDTH), lambda b, j: (0, 0)),
        pl.BlockSpec((tb, tb), lambda b, j: (0, 0)),
        pl.BlockSpec((1, GLA_WIDTH, GLA_K_WIDTH), lambda b, j: (b, 0, 0)),
    ]
    args = [pb, pb, pb, gz, wg_pad, bg, tri, s0]
    final = final_args is not None
    if final:
        o_prev, norm_g = final_args
        in_specs += [
            pl.BlockSpec((1, tb, GLA_WIDTH), lambda b, j: (b, blk(j), 0)),
            pl.BlockSpec((1, tb, GLA_WIDTH), lambda b, j: (b, blk(j), OFF_GR // GLA_WIDTH)),
            pl.BlockSpec((1, GLA_DV), lambda b, j: (0, 0)),
        ]
        args += [o_prev, pb, norm_g.reshape(1, GLA_DV)]
    return pl.pallas_call(
        functools.partial(_gla_kernel, reverse=reverse, final=final),
        grid=(bsz, nb),
        in_specs=in_specs,
        out_specs=[pl.BlockSpec((1, tb, GLA_WIDTH), lambda b, j: (b, blk(j), 0)),
                   pl.BlockSpec((1, GLA_WIDTH, GLA_K_WIDTH), lambda b, j: (b, 0, 0))],
        out_shape=[jax.ShapeDtypeStruct((bsz, length, GLA_WIDTH), BF16 if final else F32),
                   jax.ShapeDtypeStruct((bsz, GLA_WIDTH, GLA_K_WIDTH), F32)],
        scratch_shapes=[pltpu.VMEM((GLA_WIDTH, GLA_K_WIDTH), F32)],
        compiler_params=_cparams(("parallel", "arbitrary")),
        name="gla_bwd" if reverse else "gla_fwd",
    )(*args)


def _gla_consts(tb):
    tri = np.kron(np.eye(tb // GLA_CHUNK), np.tril(np.ones((GLA_CHUNK, GLA_CHUNK)))).astype(np.float32)
    return jnp.asarray(tri).astype(BF16), jnp.asarray(tri.T.copy()).astype(BF16)


def _slab_rows(n_slabs):
    return n_slabs * SLAB_PITCH


def _store_slabs(ref, lead, val, n_slabs):
    rows, lanes = val.shape
    for s in range(n_slabs):
        base = s * SLAB_PITCH
        lo = s * FFT_N1
        if lo + FFT_N1 <= rows:
            ref[lead + (slice(base, base + FFT_N1), slice(None))] = val[lo:lo + FFT_N1]
            ref[lead + (slice(base + FFT_N1, base + SLAB_PITCH), slice(None))] = jnp.zeros(
                (SLAB_PITCH - FFT_N1, lanes), val.dtype)
        else:
            ref[lead + (slice(base, base + SLAB_PITCH), slice(None))] = jnp.zeros((SLAB_PITCH, lanes), val.dtype)


def _hy_filter_kernel(feat_ref, w1_ref, b1_ref, f1_ref, w2_ref, b2_ref, f2_ref, w3_ref, dl_ref, o_ref, *, length):
    tl = feat_ref.shape[0]
    feat = feat_ref[...]
    h = jnp.sin(f1_ref[...] * (jnp.dot(feat, w1_ref[...], precision=HIGHEST, preferred_element_type=F32)
                               + b1_ref[...]))
    h = jnp.sin(f2_ref[...] * (jnp.dot(h, w2_ref[...], precision=HIGHEST, preferred_element_type=F32)
                               + b2_ref[...]))
    h = jnp.dot(h, w3_ref[...], precision=HIGHEST, preferred_element_type=F32)
    win = jnp.exp(-feat[:, 0:1] * dl_ref[...])
    pos = lax.broadcasted_iota(jnp.int32, win.shape, 0) + pl.program_id(0) * tl
    valid = pos < length
    taps_f = jnp.where(valid, h[:, :HY_WIDTH] * win, 0.0)
    taps_b = jnp.where(jnp.logical_and(valid, pos > 0), h[:, HY_WIDTH:] * win, 0.0)
    _store_slabs(o_ref, (0,), taps_f, tl // FFT_N1)
    _store_slabs(o_ref, (1,), taps_b, tl // FFT_N1)


def _hy_feats(length, lpad):
    t = jnp.linspace(0.0, 1.0, length, dtype=F32)[:, None]
    bands = (HY_POS_DIM - 1) // 2
    w = 2.0 * math.pi * jnp.arange(length, dtype=F32)[:, None] / length
    f = jnp.linspace(1e-4, bands - 1, bands, dtype=F32)[None, :]
    feat = jnp.concatenate([t, jnp.cos(f * w), -jnp.sin(f * w)], axis=-1)
    return jnp.zeros((lpad, LANES), F32).at[:length, :HY_POS_DIM].set(feat)


def _hy_filters(feat, length, p):
    lpad = feat.shape[0]
    tl = min(lpad, 512)
    max_decay = math.log(HY_TARGET) / HY_FAST_PCT
    min_decay = math.log(HY_TARGET) / HY_SLOW_PCT
    deltas = jnp.abs(jnp.linspace(min_decay, max_decay, HY_WIDTH, dtype=F32))[None, :]
    w1 = jnp.zeros((LANES, HY_HIDDEN), F32).at[:HY_POS_DIM].set(p['hy_w1'])
    full = lambda shape: pl.BlockSpec(shape, lambda i: (0,) * len(shape))
    tl_slab = _slab_rows(tl // FFT_N1)
    return pl.pallas_call(
        functools.partial(_hy_filter_kernel, length=length),
        grid=(lpad // tl,),
        in_specs=[pl.BlockSpec((tl, LANES), lambda i: (i, 0)),
                  full((LANES, HY_HIDDEN)), full((1, HY_HIDDEN)), full((1, HY_HIDDEN)),
                  full((HY_HIDDEN, HY_HIDDEN)), full((1, HY_HIDDEN)), full((1, HY_HIDDEN)),
                  full((HY_HIDDEN, 2 * HY_WIDTH)), full((1, HY_WIDTH))],
        out_specs=pl.BlockSpec((2, tl_slab, HY_WIDTH), lambda i: (0, i, 0)),
        out_shape=jax.ShapeDtypeStruct((2, _slab_rows(lpad // FFT_N1), HY_WIDTH), F32),
        compiler_params=_cparams(("parallel",)),
        name="hy_filter",
    )(feat, w1, p['hy_b1'].reshape(1, -1), p['hy_freq1'].reshape(1, -1), p['hy_w2'],
      p['hy_b2'].reshape(1, -1), p['hy_freq2'].reshape(1, -1), p['hy_w3'], deltas)


def _hy_pre_kernel(x0_ref, x1_ref, vv_ref, w0_ref, w1_ref, wv_ref, b0_ref, b1_ref, bv_ref, x0o_ref, u_ref, *, n_slabs):
    length = x0_ref.shape[1]
    row = lax.broadcasted_iota(jnp.int32, (length, x0_ref.shape[2]), 0)

    def conv(x_ref, w_ref, b_ref):
        x = x_ref[0].astype(F32)
        w = w_ref[...]
        xm = jnp.where(row == 0, 0.0, pltpu.roll(x, 1, axis=0))
        xp = jnp.where(row == length - 1, 0.0, pltpu.roll(x, length - 1, axis=0))
        return xm * w[0:1] + x * w[1:2] + xp * w[2:3] + b_ref[...]

    x0 = conv(x0_ref, w0_ref, b0_ref)
    u = conv(x1_ref, w1_ref, b1_ref) * conv(vv_ref, wv_ref, bv_ref)
    _store_slabs(x0o_ref, (0,), x0, n_slabs)
    _store_slabs(u_ref, (0,), u, n_slabs)


def _hy_pre(pb, conv_w, conv_b, n_slabs):
    bsz, length, _ = pb.shape
    nct = HY_WIDTH // LANES
    base = OFF_HY // LANES
    xs = lambda g: pl.BlockSpec((1, length, LANES), lambda b, j, g=g: (b, 0, base + g * nct + j))
    ws = lambda g: pl.BlockSpec((3, LANES), lambda b, j, g=g: (0, g * nct + j))
    bs = lambda g: pl.BlockSpec((1, LANES), lambda b, j, g=g: (0, g * nct + j))
    cb = conv_b.reshape(1, -1)
    rows = _slab_rows(n_slabs)
    return pl.pallas_call(
        functools.partial(_hy_pre_kernel, n_slabs=n_slabs),
        grid=(bsz, nct),
        in_specs=[xs(0), xs(1), xs(2), ws(0), ws(1), ws(2), bs(0), bs(1), bs(2)],
        out_specs=[pl.BlockSpec((1, rows, LANES), lambda b, j: (b, 0, j))] * 2,
        out_shape=[jax.ShapeDtypeStruct((bsz, rows, HY_WIDTH), F32)] * 2,
        compiler_params=_cparams(("parallel", "parallel")),
        name="hy_pre",
    )(pb, pb, pb, conv_w, conv_w, conv_w, cb, cb, cb)


def _fft_consts(n2, nin):
    n1 = FFT_N1
    n = n1 * n2
    nk = min(n2, -(-(n2 // 2 + 1) // 8) * 8)
    k2 = np.arange(nk)[:, None]
    m2 = np.arange(nin)[None, :]
    ang_a = 2.0 * np.pi * (k2 * m2 % n2) / n2
    fa = np.concatenate([np.cos(ang_a), -np.sin(ang_a)], axis=0)
    if nk == n2:
        wgt = np.ones((1, nk))
    else:
        kk = np.arange(nk)
        wgt = np.where((kk == 0) | (kk == n2 // 2), 1.0, np.where(kk < n2 // 2, 2.0, 0.0))[None, :]
    ga = np.concatenate([np.cos(ang_a).T * wgt, -np.sin(ang_a).T * wgt], axis=1) / n
    k1 = np.arange(n1)[None, :, None]
    j1 = np.arange(n1)[None, None, :]
    kk2 = np.arange(nk)[:, None, None]
    ang_b = 2.0 * np.pi * ((j1 * k1 * n2 + j1 * kk2) % n) / n
    mr, mi = np.cos(ang_b), -np.sin(ang_b)
    big = np.concatenate([np.concatenate([mr, -mi], axis=2), np.concatenate([mi, mr], axis=2)], axis=1)
    big_t = np.transpose(big, (0, 2, 1))
    f32 = lambda a: jnp.asarray(a.astype(np.float32))
    bf16 = lambda a: jnp.asarray(a.astype(np.float32)).astype(BF16)
    return f32(fa), f32(ga), bf16(big), bf16(big_t), nk


def _fft_a_kernel(f_ref, x_ref, o_ref):
    nin = f_ref.shape[1]
    n_out = f_ref.shape[0]
    groups = x_ref.shape[0]
    f = f_ref[...]

    def body(n1, carry):
        for g in range(groups):
            xs = x_ref[g, pl.ds(n1, nin, stride=SLAB_PITCH), :]
            o_ref[g, pl.ds(n1, n_out, stride=SLAB_PITCH), :] = jnp.dot(f, xs, preferred_element_type=F32)
        return carry

    lax.fori_loop(0, FFT_N1, body, 0, unroll=8)
    zeros = jnp.zeros((n_out, x_ref.shape[2]), F32)
    for g in range(groups):
        for r in range(FFT_N1, SLAB_PITCH):
            o_ref[g, pl.ds(r, n_out, stride=SLAB_PITCH), :] = zeros


def _fft_stage_a(x, fa):
    g, rows, ch = x.shape
    n_out, nin = fa.shape
    assert rows == _slab_rows(nin)
    return pl.pallas_call(
        _fft_a_kernel,
        grid=(g, ch // LANES),
        in_specs=[pl.BlockSpec(fa.shape, lambda b, j: (0, 0)),
                  pl.BlockSpec((1, rows, LANES), lambda b, j: (b, 0, j))],
        out_specs=pl.BlockSpec((1, _slab_rows(n_out), LANES), lambda b, j: (b, 0, j)),
        out_shape=jax.ShapeDtypeStruct((g, _slab_rows(n_out), ch), F32),
        compiler_params=_cparams(("parallel", "parallel")),
        name="fft_stage_a",
    )(fa, x)


def _fft_spec_kernel(m_ref, a_ref, o_ref):
    big = m_ref[0]
    xf = jnp.dot(big, jnp.concatenate([a_ref[0, 0, 0], a_ref[0, 1, 0]], axis=0).astype(BF16),
                 preferred_element_type=F32)
    xb = jnp.dot(big, jnp.concatenate([a_ref[1, 0, 0], a_ref[1, 1, 0]], axis=0).astype(BF16),
                 preferred_element_type=F32)
    o_ref[0, 0] = xf[:FFT_N1] + xb[:FFT_N1]
    o_ref[1, 0] = xf[FFT_N1:] - xb[FFT_N1:]


def _filter_spectrum(a_taps, big, n2):
    ch = a_taps.shape[-1]
    av = a_taps.reshape(2, 2, n2, SLAB_PITCH, ch)
    return pl.pallas_call(
        _fft_spec_kernel,
        grid=(n2,),
        in_specs=[pl.BlockSpec((1, 2 * FFT_N1, 2 * FFT_N1), lambda k: (k, 0, 0)),
                  pl.BlockSpec((2, 2, 1, FFT_N1, ch), lambda k: (0, 0, k, 0, 0))],
        out_specs=pl.BlockSpec((2, 1, FFT_N1, ch), lambda k: (0, k, 0, 0)),
        out_shape=jax.ShapeDtypeStruct((2, n2, FFT_N1, ch), F32),
        compiler_params=_cparams(("parallel",)),
        name="fft_filter_spec",
    )(big, av)


def _fft_conv_kernel(m_ref, mt_ref, h_ref, a_ref, o_ref):
    hr, hi = h_ref[0, 0], h_ref[1, 0]
    for b in range(a_ref.shape[0]):
        x = jnp.dot(m_ref[0], jnp.concatenate([a_ref[b, 0, 0], a_ref[b, 1, 0]], axis=0).astype(BF16),
                    preferred_element_type=F32)
        xr, xi = x[:FFT_N1], x[FFT_N1:]
        y = jnp.concatenate([xr * hr - xi * hi, xr * hi + xi * hr], axis=0).astype(BF16)
        t = jnp.dot(mt_ref[0], y, preferred_element_type=F32)
        pad = jnp.zeros((SLAB_PITCH - FFT_N1, t.shape[1]), F32)
        o_ref[b, 0, 0, 0:FFT_N1, :] = t[:FFT_N1]
        o_ref[b, 1, 0, 0:FFT_N1, :] = t[FFT_N1:]
        o_ref[b, 0, 0, FFT_N1:SLAB_PITCH, :] = pad
        o_ref[b, 1, 0, FFT_N1:SLAB_PITCH, :] = pad


def _fft_conv_mid(a_sig, spec, big, big_t, n2):
    bsz = a_sig.shape[0]
    ch = a_sig.shape[-1]
    av = a_sig.reshape(bsz, 2, n2, SLAB_PITCH, ch)
    out = pl.pallas_call(
        _fft_conv_kernel,
        grid=(n2,),
        in_specs=[pl.BlockSpec((1, 2 * FFT_N1, 2 * FFT_N1), lambda k: (k, 0, 0)),
                  pl.BlockSpec((1, 2 * FFT_N1, 2 * FFT_N1), lambda k: (k, 0, 0)),
                  pl.BlockSpec((2, 1, FFT_N1, ch), lambda k: (0, k, 0, 0)),
                  pl.BlockSpec((bsz, 2, 1, FFT_N1, ch), lambda k: (0, 0, k, 0, 0))],
        out_specs=pl.BlockSpec((bsz, 2, 1, SLAB_PITCH, ch), lambda k: (0, 0, k, 0, 0)),
        out_shape=jax.ShapeDtypeStruct((bsz, 2, n2, SLAB_PITCH, ch), F32),
        compiler_params=_cparams(("parallel",)),
        name="fft_conv_mid",
    )(big, big_t, spec, av)
    return out.reshape(bsz, 2 * n2 * SLAB_PITCH, ch)


def _fft_out_kernel(g_ref, t_ref, x0_ref, u_ref, skip_ref, o_ref):
    nin, n_in = g_ref.shape
    g = g_ref[...]
    skip = skip_ref[...]

    def body(n1, carry):
        ts = t_ref[0, pl.ds(n1, n_in, stride=SLAB_PITCH), :]
        y = jnp.dot(g, ts, preferred_element_type=F32)
        x0 = x0_ref[0, pl.ds(n1, nin, stride=SLAB_PITCH), :]
        u = u_ref[0, pl.ds(n1, nin, stride=SLAB_PITCH), :]
        o_ref[0, pl.ds(n1, nin, stride=FFT_N1), :] = x0 * (y + skip * u)
        return carry

    lax.fori_loop(0, FFT_N1, body, 0, unroll=8)


def _fft_out(t, ga, x0, u, skip):
    bsz, rows, ch = u.shape
    nin, n_in = ga.shape
    return pl.pallas_call(
        _fft_out_kernel,
        grid=(bsz, ch // LANES),
        in_specs=[pl.BlockSpec(ga.shape, lambda b, j: (0, 0)),
                  pl.BlockSpec((1, _slab_rows(n_in), LANES), lambda b, j: (b, 0, j)),
                  pl.BlockSpec((1, rows, LANES), lambda b, j: (b, 0, j)),
                  pl.BlockSpec((1, rows, LANES), lambda b, j: (b, 0, j)),
                  pl.BlockSpec((1, LANES), lambda b, j: (0, j))],
        out_specs=pl.BlockSpec((1, nin * FFT_N1, LANES), lambda b, j: (b, 0, j)),
        out_shape=jax.ShapeDtypeStruct((bsz, nin * FFT_N1, ch), F32),
        compiler_params=_cparams(("parallel", "parallel")),
        name="fft_out",
    )(ga, t, x0, u, skip.reshape(1, ch))


def _hyena(pb, p, consts, length):
    fa, ga, big, big_t, nk, feat, nin = consts
    taps = _hy_filters(feat, length, p)
    spec = _filter_spectrum(_fft_stage_a(taps, fa), big, nk)
    x0, u = _hy_pre(pb, p['hy_conv_w'], p['hy_conv_b'], nin)
    t = _fft_conv_mid(_fft_stage_a(u, fa), spec, big, big_t, nk)
    y = _fft_out(t, ga, x0, u, p['hy_skip'])
    return y if y.shape[1] == length else y[:, :length]


def _hyena_consts(length):
    if length >= 2048:
        lpad, n2 = length, 2 * length // FFT_N1
    else:
        lpad = max(4 * length, 1024)
        n2 = lpad // FFT_N1
    nin = lpad // FFT_N1
    return _fft_consts(n2, nin) + (_hy_feats(length, lpad), nin)


def _merge_kernel(ya_ref, yb_ref, yc_ref, lg_ref, x_ref, gate_ref, lng_ref, lnb_ref, wb_ref, wo_ref, o_ref):
    lg = lg_ref[0].astype(F32)
    ys = (ya_ref[0], yb_ref[0], yc_ref[0].astype(BF16))
    mix = None
    for g in range(N_BRANCH):
        proj = jnp.dot(ys[g], wb_ref[g], preferred_element_type=F32)
        term = jax.nn.sigmoid(lg[:, g * D_MODEL:(g + 1) * D_MODEL]) * proj
        mix = term if mix is None else mix + term
    y = jnp.dot(mix.astype(BF16), wo_ref[...], preferred_element_type=F32)
    z = DEEPNORM_ALPHA * x_ref[0] + gate_ref[0] * y
    o_ref[0] = _layer_norm_f32(z) * lng_ref[...] + lnb_ref[...]


def _merge(ya, yb, yc, pb, x, gate, ln_g, ln_b, w_branch, w_out):
    bsz, length, d = x.shape
    tm = min(length, 512)
    return pl.pallas_call(
        _merge_kernel,
        grid=(bsz, length // tm),
        in_specs=[pl.BlockSpec((1, tm, 512), lambda b, i: (b, i, 0)),
                  pl.BlockSpec((1, tm, 512), lambda b, i: (b, i, 0)),
                  pl.BlockSpec((1, tm, 512), lambda b, i: (b, i, 0)),
                  pl.BlockSpec((1, tm, N_BRANCH * d), lambda b, i: (b, i, OFF_GATE // (N_BRANCH * d))),
                  pl.BlockSpec((1, tm, d), lambda b, i: (b, i, 0)),
                  pl.BlockSpec((1, 1, d), lambda b, i: (b, 0, 0)),
                  pl.BlockSpec((1, d), lambda b, i: (0, 0)),
                  pl.BlockSpec((1, d), lambda b, i: (0, 0)),
                  pl.BlockSpec((N_BRANCH, 512, d), lambda b, i: (0, 0, 0)),
                  pl.BlockSpec((d, d), lambda b, i: (0, 0))],
        out_specs=pl.BlockSpec((1, tm, d), lambda b, i: (b, i, 0)),
        out_shape=jax.ShapeDtypeStruct((bsz, length, d), F32),
        compiler_params=_cparams(("parallel", "parallel")),
        name="merge",
    )(ya, yb, yc, pb, x, gate, ln_g.reshape(1, d), ln_b.reshape(1, d), w_branch, w_out)


FF_CHUNK = 256


def _gelu_tanh(x):
    return 0.5 * x * (1.0 + jnp.tanh(math.sqrt(2.0 / math.pi) * (x + 0.044715 * (x * x * x))))


def _ffn_kernel(xp_ref, x_ref, xn_ref, sh_ref, sc_ref, gate_ref, lng_ref, lnb_ref,
                wup_ref, cw_ref, cb_ref, wdn_ref, o_ref, h_scr, acc_scr):
    i = pl.program_id(1)
    nt = pl.num_programs(1)
    tm = x_ref.shape[1]
    sc, sh = 1.0 + sc_ref[0], sh_ref[0]
    x = x_ref[0]
    h_scr[0:8, :] = _layer_norm_f32(xp_ref[0]) * sc + sh
    h_scr[8:8 + tm, :] = _layer_norm_f32(x) * sc + sh
    h_scr[8 + tm:16 + tm, :] = _layer_norm_f32(xn_ref[0]) * sc + sh
    hext = h_scr[...].astype(BF16)

    row = lax.broadcasted_iota(jnp.int32, (tm, FF_CHUNK), 0)
    keep_prev = jnp.logical_or(row > 0, i > 0)
    keep_next = jnp.logical_or(row < tm - 1, i < nt - 1)

    def conv(up, w, b):
        um = jnp.where(keep_prev, up[7:7 + tm], 0.0)
        un = jnp.where(keep_next, up[9:9 + tm], 0.0)
        return um * w[0:1] + up[8:8 + tm] * w[1:2] + un * w[2:3] + b

    def up_pair(cidx):
        ca = cidx * FF_CHUNK
        cg = D_FF + cidx * FF_CHUNK
        return (jnp.dot(hext, wup_ref[:, ca:ca + FF_CHUNK], preferred_element_type=F32),
                jnp.dot(hext, wup_ref[:, cg:cg + FF_CHUNK], preferred_element_type=F32))

    n_chunks = D_FF // FF_CHUNK
    nxt = up_pair(0)
    for cidx in range(n_chunks):
        ca = cidx * FF_CHUNK
        cg = D_FF + cidx * FF_CHUNK
        up_a, up_g = nxt
        if cidx + 1 < n_chunks:
            nxt = up_pair(cidx + 1)
        a = conv(up_a, cw_ref[:, ca:ca + FF_CHUNK], cb_ref[:, ca:ca + FF_CHUNK])
        g = conv(up_g, cw_ref[:, cg:cg + FF_CHUNK], cb_ref[:, cg:cg + FF_CHUNK])
        act = (_gelu_tanh(g) * a).astype(BF16)
        part = jnp.dot(act, wdn_ref[ca:ca + FF_CHUNK, :], preferred_element_type=F32)
        if cidx == 0:
            acc_scr[...] = part
        else:
            acc_scr[...] += part
    z = DEEPNORM_ALPHA * x + gate_ref[0] * acc_scr[...]
    o_ref[0] = _layer_norm_f32(z) * lng_ref[...] + lnb_ref[...]


def _ffn(x, shift, scale, gate, ln_g, ln_b, w_up, conv_w, conv_b, w_down):
    bsz, length, d = x.shape
    tm = min(length, 512)
    nt = length // tm
    r8 = tm // 8
    last8 = length // 8 - 1
    vec = pl.BlockSpec((1, 1, d), lambda b, i: (b, 0, 0))
    row = pl.BlockSpec((1, d), lambda b, i: (0, 0))
    return pl.pallas_call(
        _ffn_kernel,
        grid=(bsz, nt),
        in_specs=[pl.BlockSpec((1, 8, d), lambda b, i: (b, jnp.maximum(i * r8 - 1, 0), 0)),
                  pl.BlockSpec((1, tm, d), lambda b, i: (b, i, 0)),
                  pl.BlockSpec((1, 8, d), lambda b, i: (b, jnp.minimum((i + 1) * r8, last8), 0)),
                  vec, vec, vec, row, row,
                  pl.BlockSpec((d, 2 * D_FF), lambda b, i: (0, 0)),
                  pl.BlockSpec((3, 2 * D_FF), lambda b, i: (0, 0)),
                  pl.BlockSpec((1, 2 * D_FF), lambda b, i: (0, 0)),
                  pl.BlockSpec((D_FF, d), lambda b, i: (0, 0))],
        out_specs=pl.BlockSpec((1, tm, d), lambda b, i: (b, i, 0)),
        out_shape=jax.ShapeDtypeStruct((bsz, length, d), F32),
        scratch_shapes=[pltpu.VMEM((tm + 16, d), F32), pltpu.VMEM((tm, d), F32)],
        compiler_params=_cparams(("parallel", "parallel")),
        name="conv_ffn",
    )(x, x, x, shift, scale, gate, ln_g.reshape(1, d), ln_b.reshape(1, d),
      w_up, conv_w, conv_b.reshape(1, -1), w_down)


def _prep_w_in(w_in):
    a_q, a_k, a_v, g_q, g_k, g_v, g_r, g_z, hy, gate = jnp.split(
        w_in, [512, 1024, 1536, 1792, 2048, 2560, 3072, 3104, 4640], axis=-1)
    wb = jnp.concatenate([a_q, a_k, a_v, g_q, g_k, g_v, g_r, gate, hy], axis=-1).astype(BF16)
    wz = jnp.concatenate([g_z, jnp.zeros((w_in.shape[0], LANES - 2 * GLA_GATE_RANK), w_in.dtype)],
                         axis=-1).astype(BF16)
    return wb, wz


def _prep_gate(w_gate, b_gate):
    pads = []
    for d in range(2):
        m = jnp.zeros((LANES, GLA_K_WIDTH), F32).at[d * GLA_GATE_RANK:(d + 1) * GLA_GATE_RANK].set(w_gate[d])
        pads.append(m)
    return pads, [b_gate[0].reshape(1, -1), b_gate[1].reshape(1, -1)]


def kernel(x, c, ctx, c_ctx, w_ada, b_ada, w_in, da_lambda, da_norm_g, gla_w_gate, gla_b_gate, gla_norm_g,
           hy_conv_w, hy_conv_b, hy_w1, hy_b1, hy_freq1, hy_w2, hy_b2, hy_freq2, hy_w3, hy_skip, w_branch,
           w_out, ln1_g, ln1_b, ffn_w_up, ffn_conv_w, ffn_conv_b, ffn_w_down, ln2_g, ln2_b):
    bsz, length, d = x.shape
    c_len = ctx.shape[1]
    depth = w_ada.shape[0]

    cc = jnp.zeros((8, d), F32).at[:bsz].set(c).at[bsz].set(c_ctx)
    mod = _ada_all(cc, w_ada, b_ada)

    rope = _rope_tables(length)
    hy_lat = _hyena_consts(length)
    hy_ctx = _hyena_consts(c_len)
    tri_lat = _gla_consts(min(length, 512))
    tri_ctx = _gla_consts(min(c_len, 512))
    zero_state = jnp.zeros((bsz, GLA_WIDTH, GLA_K_WIDTH), F32)

    x_lat, x_ctx = x, ctx
    for l in range(depth):
        with_ctx = l < depth - 1
        lam_init = 0.8 - 0.6 * math.exp(-0.3 * l)
        p = {'hy_conv_w': hy_conv_w[l], 'hy_conv_b': hy_conv_b[l], 'hy_w1': hy_w1[l], 'hy_b1': hy_b1[l],
             'hy_freq1': hy_freq1[l], 'hy_w2': hy_w2[l], 'hy_b2': hy_b2[l], 'hy_freq2': hy_freq2[l],
             'hy_w3': hy_w3[l], 'hy_skip': hy_skip[l]}
        wb, wz = _prep_w_in(w_in[l])
        wg_pad, bg = _prep_gate(gla_w_gate[l], gla_b_gate[l])
        wbr = w_branch[l].astype(BF16)
        wo = w_out[l].astype(BF16)
        wup = ffn_w_up[l].astype(BF16)
        wdn = ffn_w_down[l].astype(BF16)

        m_lat = mod[l, :bsz].reshape(bsz, 1, 6 * d)
        m_ctx = jnp.broadcast_to(mod[l, bsz].reshape(1, 1, 6 * d), (bsz, 1, 6 * d))
        sh1, sc1, g1, sh2, sc2, g2 = [m_lat[..., k * d:(k + 1) * d] for k in range(6)]
        csh1, csc1, cg1, csh2, csc2, cg2 = [m_ctx[..., k * d:(k + 1) * d] for k in range(6)]

        pb, gz = _inproj(x_lat, sh1, sc1, wb, wz)
        cpb, cgz = _inproj(x_ctx, csh1, csc1, wb, wz)

        y_a = _diff_attention(pb, [cpb, pb], da_lambda[l], da_norm_g[l], lam_init, rope=rope)
        co_f, cs_f = _gla_pass(cpb, cgz, wg_pad[0], bg[0], tri_ctx[0], zero_state, False)
        if with_ctx:
            y_cb, cs_b = _gla_pass(cpb, cgz, wg_pad[1], bg[1], tri_ctx[1], zero_state, True,
                                   final_args=(co_f, gla_norm_g[l]))
        else:
            _, cs_b = _gla_pass(cpb, cgz, wg_pad[1], bg[1], tri_ctx[1], zero_state, True)
        o_f, _ = _gla_pass(pb, gz, wg_pad[0], bg[0], tri_lat[0], cs_f, False)
        y_b, _ = _gla_pass(pb, gz, wg_pad[1], bg[1], tri_lat[1], cs_b, True, final_args=(o_f, gla_norm_g[l]))
        y_c = _hyena(pb, p, hy_lat, length)

        x_lat = _merge(y_a, y_b, y_c, pb, x_lat, g1, ln1_g[l], ln1_b[l], wbr, wo)
        x_lat = _ffn(x_lat, sh2, sc2, g2, ln2_g[l], ln2_b[l], wup, ffn_conv_w[l], ffn_conv_b[l], wdn)
        if with_ctx:
            y_ca = _diff_attention(cpb, [cpb], da_lambda[l], da_norm_g[l], lam_init)
            y_cc = _hyena(cpb, p, hy_ctx, c_len)
            x_ctx = _merge(y_ca, y_cb, y_cc, cpb, x_ctx, cg1, ln1_g[l], ln1_b[l], wbr, wo)
            x_ctx = _ffn(x_ctx, csh2, csc2, cg2, ln2_g[l], ln2_b[l], wup, ffn_conv_w[l], ffn_conv_b[l], wdn)
    return x_lat
```

```python
import functools
import math

import numpy as np
import jax
import jax.numpy as jnp
from jax import lax
from jax.experimental import pallas as pl
from jax.experimental.pallas import tpu as pltpu

F32 = jnp.float32
BF16 = jnp.bfloat16
HIGHEST = lax.Precision.HIGHEST

D_MODEL = 1024
DEPTH = 4
GRID_W = 64
DA_HEADS = 4
DA_HEAD_DIM = 64
DA_V_DIM = 128
ROPE_THETA = 10000.0
GLA_HEADS = 4
GLA_DK = 64
GLA_DV = 128
GLA_K_WIDTH = 256
GLA_WIDTH = 512
GLA_GATE_RANK = 16
GLA_GATE_TAU = 16.0
GLA_CHUNK = 64
HY_WIDTH = 512
HY_POS_DIM = 33
HY_HIDDEN = 64
HY_TARGET = 1e-2
HY_FAST_PCT = 0.3
HY_SLOW_PCT = 1.5
N_BRANCH = 3
D_FF = 2816
LN_EPS = 1e-5
DEEPNORM_ALPHA = (2.0 * DEPTH) ** 0.25

NB_COLS = 7680
OFF_AQ, OFF_AK, OFF_AV, OFF_GQ, OFF_GK, OFF_GV, OFF_GR, OFF_GATE, OFF_HY = (
    0, 512, 1024, 1536, 1792, 2048, 2560, 3072, 6144)

LANES = 128
FFT_N1 = 128
SLAB_PITCH = 136
VMEM_LIMIT = 56 * 1024 * 1024


def _cparams(sem):
    return pltpu.CompilerParams(dimension_semantics=sem, vmem_limit_bytes=VMEM_LIMIT)


def _layer_norm_f32(x):
    mu = jnp.mean(x, axis=-1, keepdims=True)
    xc = x - mu
    var = jnp.mean(xc * xc, axis=-1, keepdims=True)
    return xc * lax.rsqrt(var + LN_EPS)


def _ada_kernel(c_ref, w_ref, b_ref, o_ref):
    cc = c_ref[...]
    s = cc * jax.nn.sigmoid(cc)
    o_ref[0] = jnp.dot(s, w_ref[0], precision=HIGHEST, preferred_element_type=F32) + b_ref[0]


def _ada_all(cc, w_ada, b_ada):
    depth, d, n = w_ada.shape
    tn = 1536
    return pl.pallas_call(
        _ada_kernel,
        grid=(depth, n // tn),
        in_specs=[
            pl.BlockSpec((8, d), lambda l, j: (0, 0)),
            pl.BlockSpec((1, d, tn), lambda l, j: (l, 0, j)),
            pl.BlockSpec((1, 1, tn), lambda l, j: (l, 0, j)),
        ],
        out_specs=pl.BlockSpec((1, 8, tn), lambda l, j: (l, 0, j)),
        out_shape=jax.ShapeDtypeStruct((depth, 8, n), F32),
        compiler_params=_cparams(("parallel", "parallel")),
        name="ada",
    )(cc, w_ada, b_ada.reshape(depth, 1, n))


def _inproj_kernel(x_ref, sh_ref, sc_ref, w_ref, wz_ref, o_ref, z_ref, h_scr):
    @pl.when(pl.program_id(2) == 0)
    def _():
        h = (_layer_norm_f32(x_ref[0]) * (1.0 + sc_ref[0]) + sh_ref[0]).astype(BF16)
        h_scr[...] = h
        z_ref[0] = jnp.dot(h, wz_ref[...], preferred_element_type=F32)

    o_ref[0] = jnp.dot(h_scr[...], w_ref[...], preferred_element_type=F32).astype(o_ref.dtype)


def _inproj(x, shift, scale, w, wz):
    bsz, length, d = x.shape
    n_cols = w.shape[1]
    tm = min(length, 1024)
    tn = 1536
    return pl.pallas_call(
        _inproj_kernel,
        grid=(bsz, length // tm, n_cols // tn),
        in_specs=[
            pl.BlockSpec((1, tm, d), lambda b, i, j: (b, i, 0)),
            pl.BlockSpec((1, 1, d), lambda b, i, j: (b, 0, 0)),
            pl.BlockSpec((1, 1, d), lambda b, i, j: (b, 0, 0)),
            pl.BlockSpec((d, tn), lambda b, i, j: (0, j)),
            pl.BlockSpec((d, LANES), lambda b, i, j: (0, 0)),
        ],
        out_specs=[pl.BlockSpec((1, tm, tn), lambda b, i, j: (b, i, j)),
                   pl.BlockSpec((1, tm, LANES), lambda b, i, j: (b, i, 0))],
        out_shape=[jax.ShapeDtypeStruct((bsz, length, n_cols), BF16),
                   jax.ShapeDtypeStruct((bsz, length, LANES), F32)],
        scratch_shapes=[pltpu.VMEM((tm, d), BF16)],
        compiler_params=_cparams(("parallel", "parallel", "arbitrary")),
        name="inproj",
    )(x, shift, scale, w, wz)


def _rope_tables(length):
    n_freq = DA_HEAD_DIM // 4
    inv = ROPE_THETA ** (-jnp.arange(n_freq, dtype=F32) / n_freq)
    rows = jnp.repeat(jnp.arange(length // GRID_W), GRID_W).astype(F32)
    cols = (jnp.arange(length) % GRID_W).astype(F32)
    ang_r = rows[:, None] * inv
    ang_c = cols[:, None] * inv
    cos64 = jnp.concatenate([jnp.cos(ang_r), jnp.cos(ang_r), jnp.cos(ang_c), jnp.cos(ang_c)], axis=-1)
    sin64 = jnp.concatenate([-jnp.sin(ang_r), jnp.sin(ang_r), -jnp.sin(ang_c), jnp.sin(ang_c)], axis=-1)
    return jnp.concatenate([cos64, cos64], axis=-1), jnp.concatenate([sin64, sin64], axis=-1)


def _swap16(x):
    n = x.shape[-1]
    lane = lax.broadcasted_iota(jnp.int32, x.shape, x.ndim - 1)
    up = pltpu.roll(x, n - 16, axis=x.ndim - 1)
    dn = pltpu.roll(x, 16, axis=x.ndim - 1)
    return jnp.where((lane & 16) == 0, up, dn)


def _rope(x, cos, sin):
    return x * cos + _swap16(x) * sin


ATTN_TK = 512
ROPE_ROWS = 512


def _attn_kernel(*refs, n_src, rope, lam_init):
    lam_ref, g_ref, q_ref = refs[:3]
    pos = 3
    if rope:
        qcos_ref, qsin_ref, kcos_ref, ksin_ref = refs[pos:pos + 4]
        pos += 4
    kv_refs = refs[pos:pos + 2 * n_src]
    o_ref = refs[pos + 2 * n_src]
    scr = refs[pos + 2 * n_src + 1:]
    vx_scr = scr[:n_src]
    krot_scr = scr[n_src] if rope else None
    tq = q_ref.shape[1]

    @pl.when(pl.program_id(2) == 0)
    def _():
        for s_idx in range(n_src):
            v_ref = kv_refs[2 * s_idx + 1]
            n_keys = v_ref.shape[1]
            vx_scr[s_idx][:, 0:DA_V_DIM] = v_ref[0]
            vx_scr[s_idx][:, DA_V_DIM:2 * DA_V_DIM] = jnp.ones((n_keys, DA_V_DIM), BF16)
        if rope:
            k_ref = kv_refs[2 * (n_src - 1)]
            rr = min(ROPE_ROWS, k_ref.shape[1])

            def rot(j, carry):
                r0 = pl.multiple_of(j * rr, rr)
                kk = k_ref[0, pl.ds(r0, rr), :].astype(F32)
                krot_scr[pl.ds(r0, rr), :] = _rope(kk, kcos_ref[pl.ds(r0, rr), :],
                                                   ksin_ref[pl.ds(r0, rr), :]).astype(BF16)
                return carry

            lax.fori_loop(0, k_ref.shape[1] // rr, rot, 0)

    lp = lam_ref[...]
    lam = (jnp.exp(jnp.sum(lp[0] * lp[1], axis=-1, keepdims=True))
           - jnp.exp(jnp.sum(lp[2] * lp[3], axis=-1, keepdims=True)) + lam_init)

    q = q_ref[0].astype(F32)
    if rope:
        q = _rope(q, qcos_ref[...], qsin_ref[...])
    q = (q * (DA_HEAD_DIM ** -0.5 * math.log2(math.e))).astype(BF16)
    lane = lax.broadcasted_iota(jnp.int32, q.shape, 1)
    zero = jnp.zeros_like(q)
    qs = jnp.concatenate([jnp.where(lane < DA_HEAD_DIM, q, zero),
                          jnp.where(lane >= DA_HEAD_DIM, q, zero)], axis=0)

    m = jnp.full((2 * tq, 1), -1e30, F32)
    acc = jnp.zeros((2 * tq, 2 * DA_V_DIM), F32)
    tiles = []
    for s_idx in range(n_src):
        k_ref = kv_refs[2 * s_idx]
        k_src = krot_scr if (rope and s_idx == n_src - 1) else k_ref.at[0]
        n_keys = k_ref.shape[1]
        tk = min(ATTN_TK, n_keys)
        tiles += [(k_src, vx_scr[s_idx], j * tk, tk) for j in range(n_keys // tk)]

    def scores(tile):
        k_src, _, start, tk = tile
        return lax.dot_general(qs, k_src[start:start + tk, :], (((1,), (1,)), ((), ())),
                               preferred_element_type=F32)

    s_next = scores(tiles[0])
    for t_idx, (_, v_src, start, tk) in enumerate(tiles):
        s = s_next
        if t_idx + 1 < len(tiles):
            s_next = scores(tiles[t_idx + 1])
        m_new = jnp.maximum(m, jnp.max(s, axis=-1, keepdims=True))
        p = jnp.exp2(s - m_new).astype(BF16)
        acc = jnp.exp2(m - m_new) * acc + jnp.dot(p, v_src[start:start + tk, :], preferred_element_type=F32)
        m = m_new

    o = acc[:, :DA_V_DIM] / acc[:, DA_V_DIM:]
    a = o[:tq] - lam * o[tq:]
    y = a * lax.rsqrt(jnp.mean(a * a, axis=-1, keepdims=True) + LN_EPS) * g_ref[...]
    o_ref[0] = (y * (1.0 - lam_init)).astype(o_ref.dtype)


def _diff_attention(q_arr, kv_arrs, lam_p, norm_g, lam_init, rope=None):
    bsz, lq, _ = q_arr.shape
    tq = min(lq, 512)
    in_specs = [
        pl.BlockSpec((4, 1, DA_HEAD_DIM), lambda b, h, i: (0, 0, 0)),
        pl.BlockSpec((1, DA_V_DIM), lambda b, h, i: (0, 0)),
        pl.BlockSpec((1, tq, LANES), lambda b, h, i: (b, i, OFF_AQ // LANES + h)),
    ]
    args = [lam_p.reshape(4, 1, DA_HEAD_DIM), norm_g.reshape(1, DA_V_DIM), q_arr]
    if rope is not None:
        cos, sin = rope
        in_specs += [pl.BlockSpec((tq, LANES), lambda b, h, i: (i, 0))] * 2
        in_specs += [pl.BlockSpec((lq, LANES), lambda b, h, i: (0, 0))] * 2
        args += [cos, sin, cos, sin]
    scratch = []
    for arr in kv_arrs:
        lk = arr.shape[1]
        in_specs += [pl.BlockSpec((1, lk, LANES), lambda b, h, i: (b, 0, OFF_AK // LANES + h)),
                     pl.BlockSpec((1, lk, LANES), lambda b, h, i: (b, 0, OFF_AV // LANES + h))]
        args += [arr, arr]
        scratch.append(pltpu.VMEM((lk, 2 * DA_V_DIM), BF16))
    if rope is not None:
        scratch.append(pltpu.VMEM((kv_arrs[-1].shape[1], LANES), BF16))
    return pl.pallas_call(
        functools.partial(_attn_kernel, n_src=len(kv_arrs), rope=rope is not None, lam_init=lam_init),
        grid=(bsz, DA_HEADS, lq // tq),
        in_specs=in_specs,
        out_specs=pl.BlockSpec((1, tq, LANES), lambda b, h, i: (b, i, h)),
        out_shape=jax.ShapeDtypeStruct((bsz, lq, DA_HEADS * DA_V_DIM), BF16),
        scratch_shapes=scratch,
        compiler_params=_cparams(("parallel", "parallel", "arbitrary")),
        name="diff_attn",
    )(*args)


def _gla_kernel(*refs, reverse, final):
    if final:
        (q_ref, k_ref, v_ref, z_ref, wg_ref, bg_ref, tri_ref, s0_ref, oprev_ref, r_ref, g_ref,
         o_ref, sfin_ref, s_scr) = refs
    else:
        (q_ref, k_ref, v_ref, z_ref, wg_ref, bg_ref, tri_ref, s0_ref,
         o_ref, sfin_ref, s_scr) = refs
    j = pl.program_id(1)
    tb = q_ref.shape[1]
    nch = tb // GLA_CHUNK
    c = GLA_CHUNK

    @pl.when(j == 0)
    def _():
        s_scr[...] = s0_ref[0]

    pre = jnp.dot(z_ref[0], wg_ref[...], precision=HIGHEST, preferred_element_type=F32) + bg_ref[...]
    la = (jnp.minimum(pre, 0.0) - jnp.log1p(jnp.exp(-jnp.abs(pre)))) * (1.0 / GLA_GATE_TAU)
    la_hi = la.astype(BF16)
    la_lo = (la - la_hi.astype(F32)).astype(BF16)
    bsum = jnp.dot(tri_ref[...], jnp.concatenate([la_hi, la_lo], axis=-1), preferred_element_type=F32)
    b_all = bsum[:, :GLA_K_WIDTH] + bsum[:, GLA_K_WIDTH:]

    row = lax.broadcasted_iota(jnp.int32, (GLA_HEADS * c, c), 0) % c
    col = lax.broadcasted_iota(jnp.int32, (GLA_HEADS * c, c), 1)
    causal = (row <= col) if reverse else (row >= col)
    hrow = lax.broadcasted_iota(jnp.int32, (GLA_WIDTH, GLA_K_WIDTH), 0) // GLA_DV
    hcol = lax.broadcasted_iota(jnp.int32, (GLA_WIDTH, GLA_K_WIDTH), 1) // GLA_DK
    blockdiag = hrow == hcol
    qlane_head = lax.broadcasted_iota(jnp.int32, (c, GLA_K_WIDTH), 1) // GLA_DK

    state = s_scr[...]
    order = range(nch - 1, -1, -1) if reverse else range(nch)
    for ci in order:
        r0 = ci * c
        bc = b_all[r0:r0 + c]
        if reverse:
            b_end, b_ref_row = bc[0:1], bc[c - 1 - c // 2:c - c // 2]
        else:
            b_end, b_ref_row = bc[c - 1:c], bc[c // 2:c // 2 + 1]
        qc = q_ref[0, r0:r0 + c, :].astype(F32) * (GLA_DK ** -0.5)
        kc = k_ref[0, r0:r0 + c, :].astype(F32)
        vc = v_ref[0, r0:r0 + c, :]
        qe = qc * jnp.exp(bc - b_ref_row)
        ke = (kc * jnp.exp(b_ref_row - bc)).astype(BF16)
        zq = jnp.zeros_like(qe)
        q_stack = jnp.concatenate([jnp.where(qlane_head == h, qe, zq) for h in range(GLA_HEADS)],
                                  axis=0).astype(BF16)
        att = lax.dot_general(q_stack, ke, (((1,), (1,)), ((), ())), preferred_element_type=F32)
        att = jnp.where(causal, att, 0.0).astype(BF16)
        o_intra = jnp.concatenate(
            [jnp.dot(att[h * c:(h + 1) * c], vc[:, h * GLA_DV:(h + 1) * GLA_DV], preferred_element_type=F32)
             for h in range(GLA_HEADS)], axis=-1)
        o_inter = lax.dot_general((qc * jnp.exp(bc)).astype(BF16), state.astype(BF16),
                                  (((1,), (1,)), ((), ())), preferred_element_type=F32)
        o_chunk = o_intra + o_inter
        if final:
            o_chunk = o_chunk + oprev_ref[0, r0:r0 + c, :]
            parts = []
            for h in range(GLA_HEADS):
                oh = o_chunk[:, h * GLA_DV:(h + 1) * GLA_DV]
                parts.append(oh * lax.rsqrt(jnp.mean(oh * oh, axis=-1, keepdims=True) + LN_EPS) * g_ref[...])
            rr = r_ref[0, r0:r0 + c, :].astype(F32)
            o_ref[0, r0:r0 + c, :] = (jnp.concatenate(parts, axis=-1)
                                      * (rr * jax.nn.sigmoid(rr))).astype(o_ref.dtype)
        else:
            o_ref[0, r0:r0 + c, :] = o_chunk.astype(o_ref.dtype)
        kd = (kc * jnp.exp(b_end - bc)).astype(BF16)
        v_t = jnp.transpose(vc.astype(F32)).astype(BF16)
        upd = jnp.dot(v_t, kd, preferred_element_type=F32)
        state = state * jnp.exp(b_end) + jnp.where(blockdiag, upd, 0.0)
    s_scr[...] = state
    sfin_ref[0] = state


def _gla_pass(pb, gz, wg_pad, bg, tri, s0, reverse, final_args=None):
    bsz, length, _ = pb.shape
    tb = min(length, 512)
    nb = length // tb
    blk = (lambda j: nb - 1 - j) if reverse else (lambda j: j)
    in_specs = [
        pl.BlockSpec((1, tb, GLA_K_WIDTH), lambda b, j: (b, blk(j), OFF_GQ // GLA_K_WIDTH)),
        pl.BlockSpec((1, tb, GLA_K_WIDTH), lambda b, j: (b, blk(j), OFF_GK // GLA_K_WIDTH)),
        pl.BlockSpec((1, tb, GLA_WIDTH), lambda b, j: (b, blk(j), OFF_GV // GLA_WIDTH)),
        pl.BlockSpec((1, tb, LANES), lambda b, j: (b, blk(j), 0)),
        pl.BlockSpec((LANES, GLA_K_WIDTH), lambda b, j: (0, 0)),
        pl.BlockSpec((1, GLA_K_WIDTH), lambda b, j: (0, 0)),
        pl.BlockSpec((tb, tb), lambda b, j: (0, 0)),
        pl.BlockSpec((1, GLA_WIDTH, GLA_K_WIDTH), lambda b, j: (b, 0, 0)),
    ]
    args = [pb, pb, pb, gz, wg_pad, bg, tri, s0]
    final = final_args is not None
    if final:
        o_prev, norm_g = final_args
        in_specs += [
            pl.BlockSpec((1, tb, GLA_WIDTH), lambda b, j: (b, blk(j), 0)),
            pl.BlockSpec((1, tb, GLA_WIDTH), lambda b, j: (b, blk(j), OFF_GR // GLA_WIDTH)),
            pl.BlockSpec((1, GLA_DV), lambda b, j: (0, 0)),
        ]
        args += [o_prev, pb, norm_g.reshape(1, GLA_DV)]
    return pl.pallas_call(
        functools.partial(_gla_kernel, reverse=reverse, final=final),
        grid=(bsz, nb),
        in_specs=in_specs,
        out_specs=[pl.BlockSpec((1, tb, GLA_WIDTH), lambda b, j: (b, blk(j), 0)),
                   pl.BlockSpec((1, GLA_WIDTH, GLA_K_WIDTH), lambda b, j: (b, 0, 0))],
        out_shape=[jax.ShapeDtypeStruct((bsz, length, GLA_WIDTH), BF16 if final else F32),
                   jax.ShapeDtypeStruct((bsz, GLA_WIDTH, GLA_K_WIDTH), F32)],
        scratch_shapes=[pltpu.VMEM((GLA_WIDTH, GLA_K_WIDTH), F32)],
        compiler_params=_cparams(("parallel", "arbitrary")),
        name="gla_bwd" if reverse else "gla_fwd",
    )(*args)


def _gla_consts(tb):
    tri = np.kron(np.eye(tb // GLA_CHUNK), np.tril(np.ones((GLA_CHUNK, GLA_CHUNK)))).astype(np.float32)
    return jnp.asarray(tri).astype(BF16), jnp.asarray(tri.T.copy()).astype(BF16)


def _slab_rows(n_slabs):
    return n_slabs * SLAB_PITCH


def _store_slabs(ref, lead, val, n_slabs):
    rows, lanes = val.shape
    for s in range(n_slabs):
        base = s * SLAB_PITCH
        lo = s * FFT_N1
        if lo + FFT_N1 <= rows:
            ref[lead + (slice(base, base + FFT_N1), slice(None))] = val[lo:lo + FFT_N1]
            ref[lead + (slice(base + FFT_N1, base + SLAB_PITCH), slice(None))] = jnp.zeros(
                (SLAB_PITCH - FFT_N1, lanes), val.dtype)
        else:
            ref[lead + (slice(base, base + SLAB_PITCH), slice(None))] = jnp.zeros((SLAB_PITCH, lanes), val.dtype)


def _hy_filter_kernel(feat_ref, w1_ref, b1_ref, f1_ref, w2_ref, b2_ref, f2_ref, w3_ref, dl_ref, o_ref, *, length):
    tl = feat_ref.shape[0]
    feat = feat_ref[...]
    h = jnp.sin(f1_ref[...] * (jnp.dot(feat, w1_ref[...], precision=HIGHEST, preferred_element_type=F32)
                               + b1_ref[...]))
    h = jnp.sin(f2_ref[...] * (jnp.dot(h, w2_ref[...], precision=HIGHEST, preferred_element_type=F32)
                               + b2_ref[...]))
    h = jnp.dot(h, w3_ref[...], precision=HIGHEST, preferred_element_type=F32)
    win = jnp.exp(-feat[:, 0:1] * dl_ref[...])
    pos = lax.broadcasted_iota(jnp.int32, win.shape, 0) + pl.program_id(0) * tl
    valid = pos < length
    taps_f = jnp.where(valid, h[:, :HY_WIDTH] * win, 0.0)
    taps_b = jnp.where(jnp.logical_and(valid, pos > 0), h[:, HY_WIDTH:] * win, 0.0)
    _store_slabs(o_ref, (0,), taps_f, tl // FFT_N1)
    _store_slabs(o_ref, (1,), taps_b, tl // FFT_N1)


def _hy_feats(length, lpad):
    t = jnp.linspace(0.0, 1.0, length, dtype=F32)[:, None]
    bands = (HY_POS_DIM - 1) // 2
    w = 2.0 * math.pi * jnp.arange(length, dtype=F32)[:, None] / length
    f = jnp.linspace(1e-4, bands - 1, bands, dtype=F32)[None, :]
    feat = jnp.concatenate([t, jnp.cos(f * w), -jnp.sin(f * w)], axis=-1)
    return jnp.zeros((lpad, LANES), F32).at[:length, :HY_POS_DIM].set(feat)


def _hy_filters(feat, length, p):
    lpad = feat.shape[0]
    tl = min(lpad, 512)
    max_decay = math.log(HY_TARGET) / HY_FAST_PCT
    min_decay = math.log(HY_TARGET) / HY_SLOW_PCT
    deltas = jnp.abs(jnp.linspace(min_decay, max_decay, HY_WIDTH, dtype=F32))[None, :]
    w1 = jnp.zeros((LANES, HY_HIDDEN), F32).at[:HY_POS_DIM].set(p['hy_w1'])
    full = lambda shape: pl.BlockSpec(shape, lambda i: (0,) * len(shape))
    tl_slab = _slab_rows(tl // FFT_N1)
    return pl.pallas_call(
        functools.partial(_hy_filter_kernel, length=length),
        grid=(lpad // tl,),
        in_specs=[pl.BlockSpec((tl, LANES), lambda i: (i, 0)),
                  full((LANES, HY_HIDDEN)), full((1, HY_HIDDEN)), full((1, HY_HIDDEN)),
                  full((HY_HIDDEN, HY_HIDDEN)), full((1, HY_HIDDEN)), full((1, HY_HIDDEN)),
                  full((HY_HIDDEN, 2 * HY_WIDTH)), full((1, HY_WIDTH))],
        out_specs=pl.BlockSpec((2, tl_slab, HY_WIDTH), lambda i: (0, i, 0)),
        out_shape=jax.ShapeDtypeStruct((2, _slab_rows(lpad // FFT_N1), HY_WIDTH), F32),
        compiler_params=_cparams(("parallel",)),
        name="hy_filter",
    )(feat, w1, p['hy_b1'].reshape(1, -1), p['hy_freq1'].reshape(1, -1), p['hy_w2'],
      p['hy_b2'].reshape(1, -1), p['hy_freq2'].reshape(1, -1), p['hy_w3'], deltas)


def _hy_pre_kernel(x0_ref, x1_ref, vv_ref, w0_ref, w1_ref, wv_ref, b0_ref, b1_ref, bv_ref, x0o_ref, u_ref, *, n_slabs):
    length = x0_ref.shape[1]
    row = lax.broadcasted_iota(jnp.int32, (length, x0_ref.shape[2]), 0)

    def conv(x_ref, w_ref, b_ref):
        x = x_ref[0].astype(F32)
        w = w_ref[...]
        xm = jnp.where(row == 0, 0.0, pltpu.roll(x, 1, axis=0))
        xp = jnp.where(row == length - 1, 0.0, pltpu.roll(x, length - 1, axis=0))
        return xm * w[0:1] + x * w[1:2] + xp * w[2:3] + b_ref[...]

    x0 = conv(x0_ref, w0_ref, b0_ref)
    u = conv(x1_ref, w1_ref, b1_ref) * conv(vv_ref, wv_ref, bv_ref)
    _store_slabs(x0o_ref, (0,), x0, n_slabs)
    _store_slabs(u_ref, (0,), u, n_slabs)


def _hy_pre(pb, conv_w, conv_b, n_slabs):
    bsz, length, _ = pb.shape
    nct = HY_WIDTH // LANES
    base = OFF_HY // LANES
    xs = lambda g: pl.BlockSpec((1, length, LANES), lambda b, j, g=g: (b, 0, base + g * nct + j))
    ws = lambda g: pl.BlockSpec((3, LANES), lambda b, j, g=g: (0, g * nct + j))
    bs = lambda g: pl.BlockSpec((1, LANES), lambda b, j, g=g: (0, g * nct + j))
    cb = conv_b.reshape(1, -1)
    rows = _slab_rows(n_slabs)
    return pl.pallas_call(
        functools.partial(_hy_pre_kernel, n_slabs=n_slabs),
        grid=(bsz, nct),
        in_specs=[xs(0), xs(1), xs(2), ws(0), ws(1), ws(2), bs(0), bs(1), bs(2)],
        out_specs=[pl.BlockSpec((1, rows, LANES), lambda b, j: (b, 0, j))] * 2,
        out_shape=[jax.ShapeDtypeStruct((bsz, rows, HY_WIDTH), F32)] * 2,
        compiler_params=_cparams(("parallel", "parallel")),
        name="hy_pre",
    )(pb, pb, pb, conv_w, conv_w, conv_w, cb, cb, cb)


def _fft_consts(n2, nin):
    n1 = FFT_N1
    n = n1 * n2
    nk = min(n2, -(-(n2 // 2 + 1) // 8) * 8)
    k2 = np.arange(nk)[:, None]
    m2 = np.arange(nin)[None, :]
    ang_a = 2.0 * np.pi * (k2 * m2 % n2) / n2
    fa = np.concatenate([np.cos(ang_a), -np.sin(ang_a)], axis=0)
    if nk == n2:
        wgt = np.ones((1, nk))
    else:
        kk = np.arange(nk)
        wgt = np.where((kk == 0) | (kk == n2 // 2), 1.0, np.where(kk < n2 // 2, 2.0, 0.0))[None, :]
    ga = np.concatenate([np.cos(ang_a).T * wgt, -np.sin(ang_a).T * wgt], axis=1) / n
    k1 = np.arange(n1)[None, :, None]
    j1 = np.arange(n1)[None, None, :]
    kk2 = np.arange(nk)[:, None, None]
    ang_b = 2.0 * np.pi * ((j1 * k1 * n2 + j1 * kk2) % n) / n
    mr, mi = np.cos(ang_b), -np.sin(ang_b)
    big = np.concatenate([np.concatenate([mr, -mi], axis=2), np.concatenate([mi, mr], axis=2)], axis=1)
    big_t = np.transpose(big, (0, 2, 1))
    f32 = lambda a: jnp.asarray(a.astype(np.float32))
    bf16 = lambda a: jnp.asarray(a.astype(np.float32)).astype(BF16)
    return f32(fa), f32(ga), bf16(big), bf16(big_t), nk


def _fft_a_kernel(f_ref, x_ref, o_ref):
    nin = f_ref.shape[1]
    n_out = f_ref.shape[0]
    groups = x_ref.shape[0]
    f = f_ref[...]

    def body(n1, carry):
        for g in range(groups):
            xs = x_ref[g, pl.ds(n1, nin, stride=SLAB_PITCH), :]
            o_ref[g, pl.ds(n1, n_out, stride=SLAB_PITCH), :] = jnp.dot(f, xs, preferred_element_type=F32)
        return carry

    lax.fori_loop(0, FFT_N1, body, 0, unroll=8)
    zeros = jnp.zeros((n_out, x_ref.shape[2]), F32)
    for g in range(groups):
        for r in range(FFT_N1, SLAB_PITCH):
            o_ref[g, pl.ds(r, n_out, stride=SLAB_PITCH), :] = zeros


def _fft_stage_a(x, fa):
    g, rows, ch = x.shape
    n_out, nin = fa.shape
    assert rows == _slab_rows(nin)
    return pl.pallas_call(
        _fft_a_kernel,
        grid=(g, ch // LANES),
        in_specs=[pl.BlockSpec(fa.shape, lambda b, j: (0, 0)),
                  pl.BlockSpec((1, rows, LANES), lambda b, j: (b, 0, j))],
        out_specs=pl.BlockSpec((1, _slab_rows(n_out), LANES), lambda b, j: (b, 0, j)),
        out_shape=jax.ShapeDtypeStruct((g, _slab_rows(n_out), ch), F32),
        compiler_params=_cparams(("parallel", "parallel")),
        name="fft_stage_a",
    )(fa, x)


def _fft_spec_kernel(m_ref, a_ref, o_ref):
    big = m_ref[0]
    xf = jnp.dot(big, jnp.concatenate([a_ref[0, 0, 0], a_ref[0, 1, 0]], axis=0).astype(BF16),
                 preferred_element_type=F32)
    xb = jnp.dot(big, jnp.concatenate([a_ref[1, 0, 0], a_ref[1, 1, 0]], axis=0).astype(BF16),
                 preferred_element_type=F32)
    o_ref[0, 0] = xf[:FFT_N1] + xb[:FFT_N1]
    o_ref[1, 0] = xf[FFT_N1:] - xb[FFT_N1:]


def _filter_spectrum(a_taps, big, n2):
    ch = a_taps.shape[-1]
    av = a_taps.reshape(2, 2, n2, SLAB_PITCH, ch)
    return pl.pallas_call(
        _fft_spec_kernel,
        grid=(n2,),
        in_specs=[pl.BlockSpec((1, 2 * FFT_N1, 2 * FFT_N1), lambda k: (k, 0, 0)),
                  pl.BlockSpec((2, 2, 1, FFT_N1, ch), lambda k: (0, 0, k, 0, 0))],
        out_specs=pl.BlockSpec((2, 1, FFT_N1, ch), lambda k: (0, k, 0, 0)),
        out_shape=jax.ShapeDtypeStruct((2, n2, FFT_N1, ch), F32),
        compiler_params=_cparams(("parallel",)),
        name="fft_filter_spec",
    )(big, av)


def _fft_conv_kernel(m_ref, mt_ref, h_ref, a_ref, o_ref):
    hr, hi = h_ref[0, 0], h_ref[1, 0]
    for b in range(a_ref.shape[0]):
        x = jnp.dot(m_ref[0], jnp.concatenate([a_ref[b, 0, 0], a_ref[b, 1, 0]], axis=0).astype(BF16),
                    preferred_element_type=F32)
        xr, xi = x[:FFT_N1], x[FFT_N1:]
        y = jnp.concatenate([xr * hr - xi * hi, xr * hi + xi * hr], axis=0).astype(BF16)
        t = jnp.dot(mt_ref[0], y, preferred_element_type=F32)
        pad = jnp.zeros((SLAB_PITCH - FFT_N1, t.shape[1]), F32)
        o_ref[b, 0, 0, 0:FFT_N1, :] = t[:FFT_N1]
        o_ref[b, 1, 0, 0:FFT_N1, :] = t[FFT_N1:]
        o_ref[b, 0, 0, FFT_N1:SLAB_PITCH, :] = pad
        o_ref[b, 1, 0, FFT_N1:SLAB_PITCH, :] = pad


def _fft_conv_mid(a_sig, spec, big, big_t, n2):
    bsz = a_sig.shape[0]
    ch = a_sig.shape[-1]
    av = a_sig.reshape(bsz, 2, n2, SLAB_PITCH, ch)
    out = pl.pallas_call(
        _fft_conv_kernel,
        grid=(n2,),
        in_specs=[pl.BlockSpec((1, 2 * FFT_N1, 2 * FFT_N1), lambda k: (k, 0, 0)),
                  pl.BlockSpec((1, 2 * FFT_N1, 2 * FFT_N1), lambda k: (k, 0, 0)),
                  pl.BlockSpec((2, 1, FFT_N1, ch), lambda k: (0, k, 0, 0)),
                  pl.BlockSpec((bsz, 2, 1, FFT_N1, ch), lambda k: (0, 0, k, 0, 0))],
        out_specs=pl.BlockSpec((bsz, 2, 1, SLAB_PITCH, ch), lambda k: (0, 0, k, 0, 0)),
        out_shape=jax.ShapeDtypeStruct((bsz, 2, n2, SLAB_PITCH, ch), F32),
        compiler_params=_cparams(("parallel",)),
        name="fft_conv_mid",
    )(big, big_t, spec, av)
    return out.reshape(bsz, 2 * n2 * SLAB_PITCH, ch)


def _fft_out_kernel(g_ref, t_ref, x0_ref, u_ref, skip_ref, o_ref):
    nin, n_in = g_ref.shape
    g = g_ref[...]
    skip = skip_ref[...]

    def body(n1, carry):
        ts = t_ref[0, pl.ds(n1, n_in, stride=SLAB_PITCH), :]
        y = jnp.dot(g, ts, preferred_element_type=F32)
        x0 = x0_ref[0, pl.ds(n1, nin, stride=SLAB_PITCH), :]
        u = u_ref[0, pl.ds(n1, nin, stride=SLAB_PITCH), :]
        o_ref[0, pl.ds(n1, nin, stride=FFT_N1), :] = x0 * (y + skip * u)
        return carry

    lax.fori_loop(0, FFT_N1, body, 0, unroll=8)


def _fft_out(t, ga, x0, u, skip):
    bsz, rows, ch = u.shape
    nin, n_in = ga.shape
    return pl.pallas_call(
        _fft_out_kernel,
        grid=(bsz, ch // LANES),
        in_specs=[pl.BlockSpec(ga.shape, lambda b, j: (0, 0)),
                  pl.BlockSpec((1, _slab_rows(n_in), LANES), lambda b, j: (b, 0, j)),
                  pl.BlockSpec((1, rows, LANES), lambda b, j: (b, 0, j)),
                  pl.BlockSpec((1, rows, LANES), lambda b, j: (b, 0, j)),
                  pl.BlockSpec((1, LANES), lambda b, j: (0, j))],
        out_specs=pl.BlockSpec((1, nin * FFT_N1, LANES), lambda b, j: (b, 0, j)),
        out_shape=jax.ShapeDtypeStruct((bsz, nin * FFT_N1, ch), F32),
        compiler_params=_cparams(("parallel", "parallel")),
        name="fft_out",
    )(ga, t, x0, u, skip.reshape(1, ch))


def _hyena(pb, p, consts, length):
    fa, ga, big, big_t, nk, feat, nin = consts
    taps = _hy_filters(feat, length, p)
    spec = _filter_spectrum(_fft_stage_a(taps, fa), big, nk)
    x0, u = _hy_pre(pb, p['hy_conv_w'], p['hy_conv_b'], nin)
    t = _fft_conv_mid(_fft_stage_a(u, fa), spec, big, big_t, nk)
    y = _fft_out(t, ga, x0, u, p['hy_skip'])
    return y if y.shape[1] == length else y[:, :length]


def _hyena_consts(length):
    if length >= 2048:
        lpad, n2 = length, 2 * length // FFT_N1
    else:
        lpad = max(4 * length, 1024)
        n2 = lpad // FFT_N1
    nin = lpad // FFT_N1
    return _fft_consts(n2, nin) + (_hy_feats(length, lpad), nin)


def _merge_kernel(ya_ref, yb_ref, yc_ref, lg_ref, x_ref, gate_ref, lng_ref, lnb_ref, wb_ref, wo_ref, o_ref):
    lg = lg_ref[0].astype(F32)
    ys = (ya_ref[0], yb_ref[0], yc_ref[0].astype(BF16))
    mix = None
    for g in range(N_BRANCH):
        proj = jnp.dot(ys[g], wb_ref[g], preferred_element_type=F32)
        term = jax.nn.sigmoid(lg[:, g * D_MODEL:(g + 1) * D_MODEL]) * proj
        mix = term if mix is None else mix + term
    y = jnp.dot(mix.astype(BF16), wo_ref[...], preferred_element_type=F32)
    z = DEEPNORM_ALPHA * x_ref[0] + gate_ref[0] * y
    o_ref[0] = _layer_norm_f32(z) * lng_ref[...] + lnb_ref[...]


def _merge(ya, yb, yc, pb, x, gate, ln_g, ln_b, w_branch, w_out):
    bsz, length, d = x.shape
    tm = min(length, 512)
    return pl.pallas_call(
        _merge_kernel,
        grid=(bsz, length // tm),
        in_specs=[pl.BlockSpec((1, tm, 512), lambda b, i: (b, i, 0)),
                  pl.BlockSpec((1, tm, 512), lambda b, i: (b, i, 0)),
                  pl.BlockSpec((1, tm, 512), lambda b, i: (b, i, 0)),
                  pl.BlockSpec((1, tm, N_BRANCH * d), lambda b, i: (b, i, OFF_GATE // (N_BRANCH * d))),
                  pl.BlockSpec((1, tm, d), lambda b, i: (b, i, 0)),
                  pl.BlockSpec((1, 1, d), lambda b, i: (b, 0, 0)),
                  pl.BlockSpec((1, d), lambda b, i: (0, 0)),
                  pl.BlockSpec((1, d), lambda b, i: (0, 0)),
                  pl.BlockSpec((N_BRANCH, 512, d), lambda b, i: (0, 0, 0)),
                  pl.BlockSpec((d, d), lambda b, i: (0, 0))],
        out_specs=pl.BlockSpec((1, tm, d), lambda b, i: (b, i, 0)),
        out_shape=jax.ShapeDtypeStruct((bsz, length, d), F32),
        compiler_params=_cparams(("parallel", "parallel")),
        name="merge",
    )(ya, yb, yc, pb, x, gate, ln_g.reshape(1, d), ln_b.reshape(1, d), w_branch, w_out)


FF_CHUNK = 256


def _gelu_tanh(x):
    return 0.5 * x * (1.0 + jnp.tanh(math.sqrt(2.0 / math.pi) * (x + 0.044715 * (x * x * x))))


def _ffn_kernel(xp_ref, x_ref, xn_ref, sh_ref, sc_ref, gate_ref, lng_ref, lnb_ref,
                wup_ref, cw_ref, cb_ref, wdn_ref, o_ref, h_scr, acc_scr):
    i = pl.program_id(1)
    nt = pl.num_programs(1)
    tm = x_ref.shape[1]
    sc, sh = 1.0 + sc_ref[0], sh_ref[0]
    x = x_ref[0]
    h_scr[0:8, :] = _layer_norm_f32(xp_ref[0]) * sc + sh
    h_scr[8:8 + tm, :] = _layer_norm_f32(x) * sc + sh
    h_scr[8 + tm:16 + tm, :] = _layer_norm_f32(xn_ref[0]) * sc + sh
    hext = h_scr[...].astype(BF16)

    row = lax.broadcasted_iota(jnp.int32, (tm, FF_CHUNK), 0)
    keep_prev = jnp.logical_or(row > 0, i > 0)
    keep_next = jnp.logical_or(row < tm - 1, i < nt - 1)

    def conv(up, w, b):
        um = jnp.where(keep_prev, up[7:7 + tm], 0.0)
        un = jnp.where(keep_next, up[9:9 + tm], 0.0)
        return um * w[0:1] + up[8:8 + tm] * w[1:2] + un * w[2:3] + b

    def up_pair(cidx):
        ca = cidx * FF_CHUNK
        cg = D_FF + cidx * FF_CHUNK
        return (jnp.dot(hext, wup_ref[:, ca:ca + FF_CHUNK], preferred_element_type=F32),
                jnp.dot(hext, wup_ref[:, cg:cg + FF_CHUNK], preferred_element_type=F32))

    n_chunks = D_FF // FF_CHUNK
    nxt = up_pair(0)
    for cidx in range(n_chunks):
        ca = cidx * FF_CHUNK
        cg = D_FF + cidx * FF_CHUNK
        up_a, up_g = nxt
        if cidx + 1 < n_chunks:
            nxt = up_pair(cidx + 1)
        a = conv(up_a, cw_ref[:, ca:ca + FF_CHUNK], cb_ref[:, ca:ca + FF_CHUNK])
        g = conv(up_g, cw_ref[:, cg:cg + FF_CHUNK], cb_ref[:, cg:cg + FF_CHUNK])
        act = (_gelu_tanh(g) * a).astype(BF16)
        part = jnp.dot(act, wdn_ref[ca:ca + FF_CHUNK, :], preferred_element_type=F32)
        if cidx == 0:
            acc_scr[...] = part
        else:
            acc_scr[...] += part
    z = DEEPNORM_ALPHA * x + gate_ref[0] * acc_scr[...]
    o_ref[0] = _layer_norm_f32(z) * lng_ref[...] + lnb_ref[...]


def _ffn(x, shift, scale, gate, ln_g, ln_b, w_up, conv_w, conv_b, w_down):
    bsz, length, d = x.shape
    tm = min(length, 512)
    nt = length // tm
    r8 = tm // 8
    last8 = length // 8 - 1
    vec = pl.BlockSpec((1, 1, d), lambda b, i: (b, 0, 0))
    row = pl.BlockSpec((1, d), lambda b, i: (0, 0))
    return pl.pallas_call(
        _ffn_kernel,
        grid=(bsz, nt),
        in_specs=[pl.BlockSpec((1, 8, d), lambda b, i: (b, jnp.maximum(i * r8 - 1, 0), 0)),
                  pl.BlockSpec((1, tm, d), lambda b, i: (b, i, 0)),
                  pl.BlockSpec((1, 8, d), lambda b, i: (b, jnp.minimum((i + 1) * r8, last8), 0)),
                  vec, vec, vec, row, row,
                  pl.BlockSpec((d, 2 * D_FF), lambda b, i: (0, 0)),
                  pl.BlockSpec((3, 2 * D_FF), lambda b, i: (0, 0)),
                  pl.BlockSpec((1, 2 * D_FF), lambda b, i: (0, 0)),
                  pl.BlockSpec((D_FF, d), lambda b, i: (0, 0))],
        out_specs=pl.BlockSpec((1, tm, d), lambda b, i: (b, i, 0)),
        out_shape=jax.ShapeDtypeStruct((bsz, length, d), F32),
        scratch_shapes=[pltpu.VMEM((tm + 16, d), F32), pltpu.VMEM((tm, d), F32)],
        compiler_params=_cparams(("parallel", "parallel")),
        name="conv_ffn",
    )(x, x, x, shift, scale, gate, ln_g.reshape(1, d), ln_b.reshape(1, d),
      w_up, conv_w, conv_b.reshape(1, -1), w_down)


def _prep_w_in(w_in):
    a_q, a_k, a_v, g_q, g_k, g_v, g_r, g_z, hy, gate = jnp.split(
        w_in, [512, 1024, 1536, 1792, 2048, 2560, 3072, 3104, 4640], axis=-1)
    wb = jnp.concatenate([a_q, a_k, a_v, g_q, g_k, g_v, g_r, gate, hy], axis=-1).astype(BF16)
    wz = jnp.concatenate([g_z, jnp.zeros((w_in.shape[0], LANES - 2 * GLA_GATE_RANK), w_in.dtype)],
                         axis=-1).astype(BF16)
    return wb, wz


def _prep_gate(w_gate, b_gate):
    pads = []
    for d in range(2):
        m = jnp.zeros((LANES, GLA_K_WIDTH), F32).at[d * GLA_GATE_RANK:(d + 1) * GLA_GATE_RANK].set(w_gate[d])
        pads.append(m)
    return pads, [b_gate[0].reshape(1, -1), b_gate[1].reshape(1, -1)]


def kernel(x, c, ctx, c_ctx, w_ada, b_ada, w_in, da_lambda, da_norm_g, gla_w_gate, gla_b_gate, gla_norm_g,
           hy_conv_w, hy_conv_b, hy_w1, hy_b1, hy_freq1, hy_w2, hy_b2, hy_freq2, hy_w3, hy_skip, w_branch,
           w_out, ln1_g, ln1_b, ffn_w_up, ffn_conv_w, ffn_conv_b, ffn_w_down, ln2_g, ln2_b):
    bsz, length, d = x.shape
    c_len = ctx.shape[1]
    depth = w_ada.shape[0]

    cc = jnp.zeros((8, d), F32).at[:bsz].set(c).at[bsz].set(c_ctx)
    mod = _ada_all(cc, w_ada, b_ada)

    rope = _rope_tables(length)
    hy_lat = _hyena_consts(length)
    hy_ctx = _hyena_consts(c_len)
    tri_lat = _gla_consts(min(length, 512))
    tri_ctx = _gla_consts(min(c_len, 512))
    zero_state = jnp.zeros((bsz, GLA_WIDTH, GLA_K_WIDTH), F32)

    x_lat, x_ctx = x, ctx
    for l in range(depth):
        with_ctx = l < depth - 1
        lam_init = 0.8 - 0.6 * math.exp(-0.3 * l)
        p = {'hy_conv_w': hy_conv_w[l], 'hy_conv_b': hy_conv_b[l], 'hy_w1': hy_w1[l], 'hy_b1': hy_b1[l],
             'hy_freq1': hy_freq1[l], 'hy_w2': hy_w2[l], 'hy_b2': hy_b2[l], 'hy_freq2': hy_freq2[l],
             'hy_w3': hy_w3[l], 'hy_skip': hy_skip[l]}
        wb, wz = _prep_w_in(w_in[l])
        wg_pad, bg = _prep_gate(gla_w_gate[l], gla_b_gate[l])
        wbr = w_branch[l].astype(BF16)
        wo = w_out[l].astype(BF16)
        wup = ffn_w_up[l].astype(BF16)
        wdn = ffn_w_down[l].astype(BF16)

        m_lat = mod[l, :bsz].reshape(bsz, 1, 6 * d)
        m_ctx = jnp.broadcast_to(mod[l, bsz].reshape(1, 1, 6 * d), (bsz, 1, 6 * d))
        sh1, sc1, g1, sh2, sc2, g2 = [m_lat[..., k * d:(k + 1) * d] for k in range(6)]
        csh1, csc1, cg1, csh2, csc2, cg2 = [m_ctx[..., k * d:(k + 1) * d] for k in range(6)]

        pb, gz = _inproj(x_lat, sh1, sc1, wb, wz)
        cpb, cgz = _inproj(x_ctx, csh1, csc1, wb, wz)

        y_a = _diff_attention(pb, [cpb, pb], da_lambda[l], da_norm_g[l], lam_init, rope=rope)
        co_f, cs_f = _gla_pass(cpb, cgz, wg_pad[0], bg[0], tri_ctx[0], zero_state, False)
        if with_ctx:
            y_cb, cs_b = _gla_pass(cpb, cgz, wg_pad[1], bg[1], tri_ctx[1], zero_state, True,
                                   final_args=(co_f, gla_norm_g[l]))
        else:
            _, cs_b = _gla_pass(cpb, cgz, wg_pad[1], bg[1], tri_ctx[1], zero_state, True)
        o_f, _ = _gla_pass(pb, gz, wg_pad[0], bg[0], tri_lat[0], cs_f, False)
        y_b, _ = _gla_pass(pb, gz, wg_pad[1], bg[1], tri_lat[1], cs_b, True, final_args=(o_f, gla_norm_g[l]))
        y_c = _hyena(pb, p, hy_lat, length)

        x_lat = _merge(y_a, y_b, y_c, pb, x_lat, g1, ln1_g[l], ln1_b[l], wbr, wo)
        x_lat = _ffn(x_lat, sh2, sc2, g2, ln2_g[l], ln2_b[l], wup, ffn_conv_w[l], ffn_conv_b[l], wdn)
        if with_ctx:
            y_ca = _diff_attention(cpb, [cpb], da_lambda[l], da_norm_g[l], lam_init)
            y_cc = _hyena(cpb, p, hy_ctx, c_len)
            x_ctx = _merge(y_ca, y_cb, y_cc, cpb, x_ctx, cg1, ln1_g[l], ln1_b[l], wbr, wo)
            x_ctx = _ffn(x_ctx, csh2, csc2, cg2, ln2_g[l], ln2_b[l], wup, ffn_conv_w[l], ffn_conv_b[l], wdn)
    return x_lat
```
